```python
import math
import jax, jax.numpy as jnp
from jax import lax
import numpy as np

D_MODEL = 1024
BATCH = 8
SEQ = 2048
DEPTH = 2
DEC_BATCH = 128
DEC_SEQ = 1
PAST_LEN = 16384
PAGE_SIZE = 128

MIX_WIDTH = D_MODEL
ML_HEADS = 4
ML_WIDTH = MIX_WIDTH // 2
ML_DK = ML_WIDTH // ML_HEADS
ML_DV = ML_DK
ML_CHUNK = 128
CONV_WIDTH = 4
HG_HEADS = 4
HG_WIDTH = MIX_WIDTH - ML_WIDTH
HG_DK = HG_WIDTH // HG_HEADS
HG_DV = HG_DK
HG_CHUNK = 64
ML_COLS = 4 * ML_WIDTH + 2 * ML_HEADS
HG_COLS = 4 * HG_WIDTH
IN_COLS = ML_COLS + HG_COLS
N_GROUPS = 4
EXPERTS_PER_GROUP = 4
N_EXPERTS = N_GROUPS * EXPERTS_PER_GROUP
TOP_K_IN_GROUP = 2
EXPERT_FF = D_MODEL // 4
DEEPNORM_ALPHA = (2 * DEPTH) ** 0.25
DEEPNORM_BETA = (8 * DEPTH) ** -0.25
LN_EPS = 1e-5
RMS_EPS = 1e-6

kernel_name = "hymba_mlstm_hgrn2_hmoe_step"

F32 = jnp.float32


def _layer_norm(x, g, b):
    xf = x.astype(F32)
    mu = jnp.mean(xf, -1, keepdims=True)
    var = jnp.mean(jnp.square(xf - mu), -1, keepdims=True)
    return ((xf - mu) * lax.rsqrt(var + LN_EPS) * g.astype(F32) + b.astype(F32)).astype(x.dtype)


def _head_rms(h, g):
    H, Dh = h.shape[1], h.shape[3]
    h = h * lax.rsqrt(jnp.mean(jnp.square(h), -1, keepdims=True) + RMS_EPS)
    return h * g.astype(F32).reshape(1, H, 1, Dh)


def _to_heads(t, n_heads):
    B, T, _ = t.shape
    return t.reshape(B, T, n_heads, -1).transpose(0, 2, 1, 3)


def _from_heads(t):
    B, H, T, Dh = t.shape
    return t.transpose(0, 2, 1, 3).reshape(B, T, H * Dh)


def _to_chunks(t, L):
    B, H, T = t.shape[:3]
    return jnp.moveaxis(t.reshape(B, H, T // L, L, *t.shape[3:]), 2, 0)


def _from_chunks(t):
    t = jnp.moveaxis(t, 0, 2)
    return t.reshape(t.shape[0], t.shape[1], -1, *t.shape[4:])


def _mlstm_chunk_step(carry, xs):
    C, n, m = carry
    q, k, v, li, lf = xs
    L = q.shape[2]
    F = jnp.cumsum(lf, axis=2)
    a = li - F
    m_t = F + jnp.maximum(m[..., None], lax.cummax(a, axis=2))
    causal = jnp.tril(jnp.ones((L, L), dtype=bool))
    log_d = (F - m_t)[..., :, None] + a[..., None, :]
    dmat = jnp.exp(jnp.where(causal, log_d, -jnp.inf))
    s = jnp.einsum('bhtd,bhsd->bhts', q, k) * dmat
    dec = jnp.exp(F + m[..., None] - m_t)
    num = jnp.einsum('bhts,bhsv->bhtv', s, v) + dec[..., None] * jnp.einsum('bhtd,bhdv->bhtv', q, C)
    qn = jnp.sum(s, -1) + dec * jnp.einsum('bhtd,bhd->bht', q, n)
    h = num / jnp.maximum(jnp.abs(qn), jnp.exp(-m_t))[..., None]
    F_end = F[..., -1]
    m_end = m_t[..., -1]
    w_end = jnp.exp(F_end[..., None] + a - m_end[..., None])
    dec_end = jnp.exp(F_end + m - m_end)
    C_new = dec_end[..., None, None] * C + jnp.einsum('bhs,bhsd,bhsv->bhdv', w_end, k, v)
    n_new = dec_end[..., None] * n + jnp.einsum('bhs,bhsd->bhd', w_end, k)
    return (C_new, n_new, m_end), h


def _mlstm(q, k, v, li, lf, C0, n0, m0):
    L = min(ML_CHUNK, q.shape[2])
    xs = tuple(_to_chunks(t, L) for t in (q, k, v, li, lf))
    (C, n, m), h = lax.scan(_mlstm_chunk_step, (C0, n0, m0), xs)
    return _from_chunks(h), C, n, m


def _hgrn2_chunk_step(S, xs):
    q, k, i, lf = xs
    L = q.shape[2]
    G = jnp.cumsum(lf, axis=2)
    causal = jnp.tril(jnp.ones((L, L), dtype=bool))[:, :, None]
    gap = G[:, :, :, None, :] - G[:, :, None, :, :]
    decay = jnp.exp(jnp.where(causal, gap, -jnp.inf))
    A = jnp.einsum('bhtc,bhtsc,bhsc->bhts', q, decay, k)
    o = jnp.einsum('bhts,bhsv->bhtv', A, i) + jnp.einsum('bhtc,bhcv->bhtv', q * jnp.exp(G), S)
    G_end = G[:, :, -1:, :]
    S_new = jnp.exp(G_end[:, :, 0, :])[..., None] * S + jnp.einsum('bhsc,bhsv->bhcv', k * jnp.exp(G_end - G), i)
    return S_new, o


def _hgrn2(q, k, i, lf, S0):
    L = min(HG_CHUNK, q.shape[2])
    xs = tuple(_to_chunks(t, L) for t in (q, k, i, lf))
    S, o = lax.scan(_hgrn2_chunk_step, S0, xs)
    return _from_chunks(o), S


def _mixer(x, conv_buf, C0, n0, m0, S0, w_in, b_in, conv_w, conv_b, ml_g, lb, hg_g, w_out):
    B, T, _ = x.shape
    z = (jnp.einsum('btd,de->bte', x, w_in) + b_in).astype(F32)
    ml, hg = z[..., :ML_COLS], z[..., ML_COLS:]
    qk_raw = ml[..., :2 * ML_WIDTH]
    v_ml = ml[..., 2 * ML_WIDTH:3 * ML_WIDTH]
    o_ml = ml[..., 3 * ML_WIDTH:4 * ML_WIDTH]
    i_pre = ml[..., 4 * ML_WIDTH:4 * ML_WIDTH + ML_HEADS]
    f_pre = ml[..., 4 * ML_WIDTH + ML_HEADS:]
    xp = jnp.concatenate([conv_buf.astype(F32), qk_raw], axis=1)
    new_buf = xp[:, T:]
    cw = conv_w.astype(F32)
    qk = conv_b.astype(F32) + xp[:, 0:T] * cw[0]
    for j in range(1, CONV_WIDTH):
        qk = qk + xp[:, j:j + T] * cw[j]
    qk = jax.nn.silu(qk)
    q = _to_heads(qk[..., :ML_WIDTH], ML_HEADS)
    k = _to_heads(qk[..., ML_WIDTH:], ML_HEADS) * (ML_DK ** -0.5)
    v = _to_heads(v_ml, ML_HEADS)
    li = i_pre.transpose(0, 2, 1)
    lf = jax.nn.log_sigmoid(f_pre).transpose(0, 2, 1)
    h, C, n, m = _mlstm(q, k, v, li, lf, C0.astype(F32), n0.astype(F32), m0.astype(F32))
    h = _head_rms(jax.nn.sigmoid(_to_heads(o_ml, ML_HEADS)) * h, ml_g)
    hq, hf, hi, hgt = jnp.split(hg, 4, axis=-1)
    lb = lb.astype(F32)
    lf_h = jnp.logaddexp(jnp.log(lb), jnp.log1p(-lb) + jax.nn.log_sigmoid(hf))
    k_h = (1.0 - lb) * jax.nn.sigmoid(-hf)
    o_h, S = _hgrn2(_to_heads(jax.nn.silu(hq), HG_HEADS), _to_heads(k_h, HG_HEADS),
                    _to_heads(hi, HG_HEADS), _to_heads(lf_h, HG_HEADS), S0.astype(F32))
    o_h = _head_rms(o_h, hg_g) * jax.nn.silu(_to_heads(hgt, HG_HEADS))
    mixed = jnp.concatenate([_from_heads(h), _from_heads(o_h)], axis=-1).astype(x.dtype)
    out = jnp.einsum('bte,ed->btd', mixed, w_out)
    return out, new_buf, C, n, m, S


def _hmoe(x, w_rg, b_rg, w_re, b_re, w1, w3, w2):
    B, T, D = x.shape
    xt = x.reshape(B * T, D)
    lg = (xt @ w_rg + b_rg).astype(F32)
    pg = jax.nn.softmax(lg, axis=-1)
    g = jnp.argmax(lg, axis=-1)
    le = (xt @ w_re + b_re).astype(F32).reshape(-1, N_GROUPS, EXPERTS_PER_GROUP)
    le_g = jnp.take_along_axis(le, g[:, None, None], axis=1)[:, 0]
    top_v, top_i = lax.top_k(le_g, TOP_K_IN_GROUP)
    wts = jax.nn.softmax(top_v, axis=-1) * jnp.take_along_axis(pg, g[:, None], axis=1)
    eid = g[:, None] * EXPERTS_PER_GROUP + top_i
    gate = jnp.einsum('nk,nke->ne', wts, jax.nn.one_hot(eid, N_EXPERTS, dtype=F32))
    a = jnp.einsum('nd,edf->nef', xt, w1)
    u = jnp.einsum('nd,edf->nef', xt, w3)
    hdn = jax.nn.silu(a) * u * gate[..., None].astype(x.dtype)
    return jnp.einsum('nef,efd->nd', hdn, w2).reshape(B, T, D)


def _layer(x, conv_buf, C, n, m, S, w_in, b_in, conv_w, conv_b, ml_g, lb, hg_g, w_out,
           ln1_g, ln1_b, w_rg, b_rg, w_re, b_re, w1, w3, w2, ln2_g, ln2_b):
    mix, conv_buf, C, n, m, S = _mixer(x, conv_buf, C, n, m, S, w_in, b_in, conv_w, conv_b,
                                       ml_g, lb, hg_g, w_out)
    x = _layer_norm(DEEPNORM_ALPHA * x + mix, ln1_g, ln1_b)
    x = _layer_norm(DEEPNORM_ALPHA * x + _hmoe(x, w_rg, b_rg, w_re, b_re, w1, w3, w2), ln2_g, ln2_b)
    return x, conv_buf, C, n, m, S


def setup_inputs(seed: int = 0) -> dict:
    key = jax.random.key(seed)
    ks = iter(jax.random.split(key, 32))

    def nrm(shape, scale):
        return scale * jax.random.normal(next(ks), shape, F32)

    col_scale = np.ones((IN_COLS,), np.float32)
    col_scale[2 * ML_WIDTH:3 * ML_WIDTH] = DEEPNORM_BETA
    col_scale[ML_COLS + 2 * HG_WIDTH:ML_COLS + 3 * HG_WIDTH] = DEEPNORM_BETA
    f_bias = np.zeros((IN_COLS,), np.float32)
    f_bias[4 * ML_WIDTH + ML_HEADS:ML_COLS] = np.linspace(3.0, 6.0, ML_HEADS)
    return {
        "x_prompt": nrm((BATCH, SEQ, D_MODEL), 1.0),
        "x_sample": nrm((DEC_BATCH, DEC_SEQ, D_MODEL), 1.0),
        "state_conv": nrm((DEPTH, DEC_BATCH, CONV_WIDTH - 1, 2 * ML_WIDTH), 1.0),
        "state_ml_C": nrm((DEPTH, DEC_BATCH, ML_HEADS, ML_DK, ML_DV), 0.3),
        "state_ml_n": nrm((DEPTH, DEC_BATCH, ML_HEADS, ML_DK), 0.3),
        "state_ml_m": nrm((DEPTH, DEC_BATCH, ML_HEADS), 1.0),
        "state_hg_S": nrm((DEPTH, DEC_BATCH, HG_HEADS, HG_DK, HG_DV), 0.5),
        "w_in": nrm((DEPTH, D_MODEL, IN_COLS), D_MODEL ** -0.5) * jnp.asarray(col_scale),
        "b_in": nrm((DEPTH, IN_COLS), 0.02) + jnp.asarray(f_bias),
        "conv_w": nrm((DEPTH, CONV_WIDTH, 2 * ML_WIDTH), CONV_WIDTH ** -0.5),
        "conv_b": nrm((DEPTH, 2 * ML_WIDTH), 0.02),
        "ml_norm_g": 1.0 + nrm((DEPTH, ML_WIDTH), 0.02),
        "hg_lower_bounds": nrm((DEPTH, HG_WIDTH), 0.5),
        "hg_norm_g": 1.0 + nrm((DEPTH, HG_WIDTH), 0.02),
        "w_out": nrm((DEPTH, MIX_WIDTH, D_MODEL), MIX_WIDTH ** -0.5 * DEEPNORM_BETA),
        "ln1_g": 1.0 + nrm((DEPTH, D_MODEL), 0.02),
        "ln1_b": nrm((DEPTH, D_MODEL), 0.02),
        "w_router_group": nrm((DEPTH, D_MODEL, N_GROUPS), D_MODEL ** -0.5),
        "b_router_group": nrm((DEPTH, N_GROUPS), 0.01),
        "w_router_expert": nrm((DEPTH, D_MODEL, N_EXPERTS), D_MODEL ** -0.5),
        "b_router_expert": nrm((DEPTH, N_EXPERTS), 0.01),
        "w_gate": nrm((DEPTH, N_EXPERTS, D_MODEL, EXPERT_FF), D_MODEL ** -0.5),
        "w_up": nrm((DEPTH, N_EXPERTS, D_MODEL, EXPERT_FF), D_MODEL ** -0.5 * DEEPNORM_BETA),
        "w_down": nrm((DEPTH, N_EXPERTS, EXPERT_FF, D_MODEL), EXPERT_FF ** -0.5 * DEEPNORM_BETA),
        "ln2_g": 1.0 + nrm((DEPTH, D_MODEL), 0.02),
        "ln2_b": nrm((DEPTH, D_MODEL), 0.02),
    }


def reference(x_prompt, x_sample, state_conv, state_ml_C, state_ml_n, state_ml_m, state_hg_S,
              w_in, b_in, conv_w, conv_b, ml_norm_g, hg_lower_bounds, hg_norm_g, w_out, ln1_g, ln1_b,
              w_router_group, b_router_group, w_router_expert, b_router_expert,
              w_gate, w_up, w_down, ln2_g, ln2_b):
    sm = jax.nn.softmax(hg_lower_bounds.astype(F32), axis=0)
    lbs = jnp.concatenate([jnp.zeros_like(sm[:1]), jnp.cumsum(sm[1:], axis=0)], axis=0)
    B = x_prompt.shape[0]
    y_p, y_s = x_prompt, x_sample
    st_p, st_s = [], []
    for l in range(DEPTH):
        p = (w_in[l], b_in[l], conv_w[l], conv_b[l], ml_norm_g[l], lbs[l], hg_norm_g[l], w_out[l],
             ln1_g[l], ln1_b[l], w_router_group[l], b_router_group[l], w_router_expert[l],
             b_router_expert[l], w_gate[l], w_up[l], w_down[l], ln2_g[l], ln2_b[l])
        zc = jnp.zeros((B, CONV_WIDTH - 1, 2 * ML_WIDTH), F32)
        zC = jnp.zeros((B, ML_HEADS, ML_DK, ML_DV), F32)
        zn = jnp.zeros((B, ML_HEADS, ML_DK), F32)
        zm = jnp.zeros((B, ML_HEADS), F32)
        zS = jnp.zeros((B, HG_HEADS, HG_DK, HG_DV), F32)
        y_p, *sp = _layer(y_p, zc, zC, zn, zm, zS, *p)
        y_s, *ss = _layer(y_s, state_conv[l], state_ml_C[l], state_ml_n[l], state_ml_m[l], state_hg_S[l], *p)
        st_p.append(sp)
        st_s.append(ss)

    def stack(states, idx, dtype):
        return jnp.stack([s[idx] for s in states]).astype(dtype)

    conv_p = stack(st_p, 0, state_conv.dtype)
    C_p = stack(st_p, 1, state_ml_C.dtype)
    n_p = stack(st_p, 2, state_ml_n.dtype)
    m_p = stack(st_p, 3, state_ml_m.dtype)
    S_p = stack(st_p, 4, state_hg_S.dtype)
    conv_s = stack(st_s, 0, state_conv.dtype)
    C_s = stack(st_s, 1, state_ml_C.dtype)
    n_s = stack(st_s, 2, state_ml_n.dtype)
    m_s = stack(st_s, 3, state_ml_m.dtype)
    S_s = stack(st_s, 4, state_hg_S.dtype)
    return (y_p, y_s, conv_p, C_p, n_p, m_p, S_p, conv_s, C_s, n_s, m_s, S_s)
```

```python
import functools

import jax
import jax.numpy as jnp
from jax import lax
from jax.experimental import pallas as pl
from jax.experimental.pallas import tpu as pltpu

F32 = jnp.float32
BF16 = jnp.bfloat16

D_MODEL = 1024
DEPTH = 2
ML_HEADS = 4
ML_WIDTH = 512
ML_DK = 128
ML_CHUNK = 128
CONV_WIDTH = 4
HG_HEADS = 4
HG_WIDTH = 512
HG_DK = 128
HG_CHUNK = 64
ML_COLS = 4 * ML_WIDTH + 2 * ML_HEADS
N_GROUPS = 4
EXPERTS_PER_GROUP = 4
N_EXPERTS = 16
EXPERT_FF = 256
DEEPNORM_ALPHA = (2 * DEPTH) ** 0.25
LN_EPS = 1e-5
RMS_EPS = 1e-6

LANES = 128
SUBLANES = 8
MAIN_COLS = 4 * ML_WIDTH + 4 * HG_WIDTH
GATE_COLS = 2 * LANES
CONV_PAD = SUBLANES
HG_SAFE_DECAY = 150.0
VMEM_LIMIT = 56 * 1024 * 1024


def _dot(a, b):
    return jnp.dot(a, b, preferred_element_type=F32)


def _dot_nt(a, b):
    return lax.dot_general(a, b, (((1,), (1,)), ((), ())), preferred_element_type=F32)


def _dot_tn(a, b):
    return lax.dot_general(a, b, (((0,), (0,)), ((), ())), preferred_element_type=F32)


def _sigmoid(x):
    return 1.0 / (1.0 + jnp.exp(-x))


def _silu(x):
    return x * _sigmoid(x)


def _log_sigmoid(x):
    return jnp.minimum(x, 0.0) - jnp.log1p(jnp.exp(-jnp.abs(x)))


def _cumsum_rows(tri_bf, x):
    hi = x.astype(BF16)
    r1 = x - hi.astype(F32)
    mid = r1.astype(BF16)
    lo = (r1 - mid.astype(F32)).astype(BF16)
    return _dot(tri_bf, hi) + _dot(tri_bf, mid) + _dot(tri_bf, lo)


def _layer_norm_rows(x, g, b):
    mu = jnp.mean(x, axis=-1, keepdims=True)
    xc = x - mu
    var = jnp.mean(xc * xc, axis=-1, keepdims=True)
    return xc * lax.rsqrt(var + LN_EPS) * g + b


def _head_rms(h, g):
    return h * lax.rsqrt(jnp.mean(h * h, axis=-1, keepdims=True) + RMS_EPS) * g


def _hg_gates(hf, log_lb, log1m_lb, one_m_lb):
    e = jnp.exp(-jnp.abs(hf))
    log_sig = jnp.minimum(hf, 0.0) - jnp.log1p(e)
    b = log1m_lb + log_sig
    hi = jnp.maximum(log_lb, b)
    lo = jnp.minimum(log_lb, b)
    lf = hi + jnp.log1p(jnp.exp(lo - hi))
    k = one_m_lb * jnp.where(hf >= 0.0, e, 1.0) / (1.0 + e)
    return lf, k


def _prompt_mixer_kernel(x_ref, w_main_ref, w_gate_ref, b_main_ref, b_gate_ref, conv_w_ref, conv_b_ref,
                         ml_g_ref, hg_g_ref, log_lb_ref, log1m_lb_ref, one_m_lb_ref, w_out_ref,
                         ln_g_ref, ln_b_ref,
                         x1_ref, conv_out_ref, caug_out_ref, m_out_ref, s_out_ref,
                         xp_scr, caug_scr, m_scr, st_scr, mixed_scr, hq_scr, hk_scr, hi_scr, hgc_scr, ho_scr,
                         *, tt):
    t = pl.program_id(1)
    nt = pl.num_programs(1)

    @pl.when(t == 0)
    def _():
        xp_scr[0:CONV_PAD, :] = jnp.zeros((CONV_PAD, 2 * ML_WIDTH), F32)
        caug_scr[...] = jnp.zeros_like(caug_scr)
        m_scr[...] = jnp.zeros_like(m_scr)
        st_scr[...] = jnp.zeros_like(st_scr)

    x = x_ref[0]
    xb = x.astype(BF16)

    def proj(lo, hi):
        return _dot(xb, w_main_ref[:, lo:hi]) + b_main_ref[:, lo:hi]

    xp_scr[CONV_PAD:CONV_PAD + tt, :] = proj(0, 2 * ML_WIDTH)
    qk = conv_b_ref[...] + xp_scr[CONV_PAD - 3:CONV_PAD - 3 + tt, :] * conv_w_ref[0:1, :]
    for j in range(1, CONV_WIDTH):
        qk = qk + xp_scr[CONV_PAD - 3 + j:CONV_PAD - 3 + j + tt, :] * conv_w_ref[j:j + 1, :]
    qk = _silu(qk)
    last_rows = xp_scr[CONV_PAD + tt - 3:CONV_PAD + tt, :]
    xp_scr[CONV_PAD - 3:CONV_PAD, :] = last_rows

    @pl.when(t == nt - 1)
    def _():
        conv_out_ref[0] = last_rows

    q_all = qk[:, :ML_WIDTH].astype(BF16)
    k_all = qk[:, ML_WIDTH:] * (ML_DK ** -0.5)
    v_all = proj(2 * ML_WIDTH, 3 * ML_WIDTH).astype(BF16)
    o_gate = _sigmoid(proj(3 * ML_WIDTH, 4 * ML_WIDTH))
    zg = _dot(xb, w_gate_ref[...]) + b_gate_ref[...]

    row = lax.broadcasted_iota(jnp.int32, (ML_CHUNK, ML_CHUNK), 0)
    col = lax.broadcasted_iota(jnp.int32, (ML_CHUNK, ML_CHUNK), 1)
    tril = row >= col
    tri_bf = jnp.where(tril, 1.0, 0.0).astype(BF16)
    ones_col = jnp.where(lax.broadcasted_iota(jnp.int32, (ML_CHUNK, LANES), 1) == 0, 1.0, 0.0).astype(BF16)

    for c in range(tt // ML_CHUNK):
        r0 = c * ML_CHUNK
        li_s = zg[r0:r0 + ML_CHUNK, 0:LANES]
        lf_s = _log_sigmoid(zg[r0:r0 + ML_CHUNK, LANES:2 * LANES])
        f_s = _cumsum_rows(tri_bf, lf_s)
        a_s = li_s - f_s
        a_t = a_s.T
        for h in range(ML_HEADS):
            hs = slice(h * ML_DK, (h + 1) * ML_DK)
            q = q_all[r0:r0 + ML_CHUNK, hs]
            k = k_all[r0:r0 + ML_CHUNK, hs]
            vaug = jnp.concatenate([v_all[r0:r0 + ML_CHUNK, hs], ones_col], axis=1)
            f_c = f_s[:, h:h + 1]
            a_c = a_s[:, h:h + 1]
            a_r = a_t[h:h + 1, :]
            m_prev = m_scr[h:h + 1, 0:1]
            cmax = jnp.max(jnp.where(tril, a_r, -jnp.inf), axis=1, keepdims=True)
            m_t = f_c + jnp.maximum(m_prev, cmax)
            dmat = jnp.where(tril, jnp.exp((f_c - m_t) + a_r), 0.0)
            s = _dot_nt(q, k.astype(BF16)) * dmat
            dec = jnp.exp(f_c + m_prev - m_t)
            caug = caug_scr[h]
            tot = _dot(s.astype(BF16), vaug) + dec * _dot(q, caug.astype(BF16))
            num = tot[:, :ML_DK]
            qn = tot[:, ML_DK:ML_DK + 1]
            hout = num / jnp.maximum(jnp.abs(qn), jnp.exp(-m_t))
            f_end = f_c[ML_CHUNK - 1:ML_CHUNK, :]
            m_end = m_t[ML_CHUNK - 1:ML_CHUNK, :]
            w_end = jnp.exp(f_end + a_c - m_end)
            dec_end = jnp.exp(f_end + m_prev - m_end)
            caug_scr[h] = dec_end * caug + _dot_tn((k * w_end).astype(BF16), vaug)
            m_scr[h:h + 1, :] = jnp.broadcast_to(m_end, (1, LANES))
            hout = _head_rms(o_gate[r0:r0 + ML_CHUNK, hs] * hout, ml_g_ref[:, hs])
            mixed_scr[r0:r0 + ML_CHUNK, hs] = hout.astype(BF16)

    base = 4 * ML_WIDTH
    hq_scr[...] = _silu(proj(base, base + HG_WIDTH))
    lf_h, k_h = _hg_gates(proj(base + HG_WIDTH, base + 2 * HG_WIDTH),
                          log_lb_ref[...], log1m_lb_ref[...], one_m_lb_ref[...])
    hk_scr[...] = k_h
    hi_scr[...] = proj(base + 2 * HG_WIDTH, base + 3 * HG_WIDTH)
    rt = lax.broadcasted_iota(jnp.int32, (tt, tt), 0)
    ct = lax.broadcasted_iota(jnp.int32, (tt, tt), 1)
    blk_tri = jnp.where((rt >= ct) & ((rt >> 6) == (ct >> 6)), 1.0, 0.0).astype(BF16)
    g_all = _cumsum_rows(blk_tri, lf_h)
    hgc_scr[...] = g_all
    tril64 = tril[:HG_CHUNK, :HG_CHUNK]
    row64 = row[:HG_CHUNK, :1]

    for c in range(tt // HG_CHUNK):
        r0 = c * HG_CHUNK
        rows = slice(r0, r0 + HG_CHUNK)
        g_end_all = hgc_scr[r0 + HG_CHUNK - 1:r0 + HG_CHUNK, :]
        safe = jnp.min(g_end_all) > -HG_SAFE_DECAY
        for h in range(HG_HEADS):
            hs = slice(h * HG_DK, (h + 1) * HG_DK)
            g = hgc_scr[rows, hs]
            q = hq_scr[rows, hs]
            k = hk_scr[rows, hs]
            iv = hi_scr[rows, hs].astype(BF16)
            g_end = g[HG_CHUNK - 1:HG_CHUNK, :]
            st = st_scr[h]
            inter = _dot_nt((q * jnp.exp(g)).astype(BF16), st.astype(BF16))

            @pl.when(safe)
            def _():
                ref = 0.5 * g_end
                qa = (q * jnp.exp(g - ref)).astype(BF16)
                kb = (k * jnp.exp(ref - g)).astype(BF16)
                amat = jnp.where(tril64, _dot_nt(qa, kb), 0.0)
                ho_scr[rows, hs] = _dot(amat.astype(BF16), iv) + inter

            @pl.when(jnp.logical_not(safe))
            def _():
                ivf = hi_scr[rows, hs]

                def body(grp, carry):
                    rows8 = pl.ds(pl.multiple_of(r0 + grp * SUBLANES, SUBLANES), SUBLANES)
                    g8 = hgc_scr[rows8, hs]
                    q8 = hq_scr[rows8, hs]
                    outs = []
                    for jj in range(SUBLANES):
                        w = jnp.exp(jnp.minimum(g8[jj:jj + 1, :] - g, 0.0)) * k * q8[jj:jj + 1, :]
                        a_col = jnp.sum(w, axis=1, keepdims=True)
                        a_col = jnp.where(row64 <= grp * SUBLANES + jj, a_col, 0.0)
                        outs.append(jnp.sum(a_col * ivf, axis=0, keepdims=True))
                    ho_scr[rows8, hs] = jnp.concatenate(outs, axis=0)
                    return carry

                lax.fori_loop(0, HG_CHUNK // SUBLANES, body, 0)
                ho_scr[rows, hs] = ho_scr[rows, hs] + inter

            st_scr[h] = jnp.exp(g_end) * st + _dot_tn(iv, (k * jnp.exp(g_end - g)).astype(BF16))

    hg_gate = _silu(proj(base + 3 * HG_WIDTH, base + 4 * HG_WIDTH))
    for h in range(HG_HEADS):
        hs = slice(h * HG_DK, (h + 1) * HG_DK)
        o = _head_rms(ho_scr[:, hs], hg_g_ref[:, hs]) * hg_gate[:, hs]
        mixed_scr[:, ML_WIDTH + h * HG_DK:ML_WIDTH + (h + 1) * HG_DK] = o.astype(BF16)

    mix = _dot(mixed_scr[...], w_out_ref[...])
    x1_ref[0] = _layer_norm_rows(DEEPNORM_ALPHA * x + mix, ln_g_ref[...], ln_b_ref[...])

    @pl.when(t == nt - 1)
    def _():
        for h in range(ML_HEADS):
            caug_out_ref[0, h] = caug_scr[h]
            s_out_ref[0, h] = st_scr[h].T
        m_out_ref[0] = m_scr[...]


def _const_spec(shape):
    nd = len(shape)
    return pl.BlockSpec(shape, lambda *_: (0,) * nd, pipeline_mode=pl.Buffered(1))


def _prompt_mixer(x, p, tt):
    bsz, seq, _ = x.shape
    consts = (p["w_main"], p["w_gate"], p["b_main"], p["b_gate"], p["conv_w"], p["conv_b"], p["ml_g"],
              p["hg_g"], p["log_lb"], p["log1m_lb"], p["one_m_lb"], p["w_out"], p["ln1_g"], p["ln1_b"])
    out_shape = (
        jax.ShapeDtypeStruct((bsz, seq, D_MODEL), F32),
        jax.ShapeDtypeStruct((bsz, CONV_WIDTH - 1, 2 * ML_WIDTH), F32),
        jax.ShapeDtypeStruct((bsz, ML_HEADS, ML_DK, 2 * ML_DK), F32),
        jax.ShapeDtypeStruct((bsz, SUBLANES, LANES), F32),
        jax.ShapeDtypeStruct((bsz, HG_HEADS, HG_DK, HG_DK), F32),
    )
    out_specs = (
        pl.BlockSpec((1, tt, D_MODEL), lambda b, t: (b, t, 0)),
        pl.BlockSpec((1, CONV_WIDTH - 1, 2 * ML_WIDTH), lambda b, t: (b, 0, 0)),
        pl.BlockSpec((1, ML_HEADS, ML_DK, 2 * ML_DK), lambda b, t: (b, 0, 0, 0)),
        pl.BlockSpec((1, SUBLANES, LANES), lambda b, t: (b, 0, 0)),
        pl.BlockSpec((1, HG_HEADS, HG_DK, HG_DK), lambda b, t: (b, 0, 0, 0)),
    )
    scratch = [
        pltpu.VMEM((CONV_PAD + tt, 2 * ML_WIDTH), F32),
        pltpu.VMEM((ML_HEADS, ML_DK, 2 * ML_DK), F32),
        pltpu.VMEM((SUBLANES, LANES), F32),
        pltpu.VMEM((HG_HEADS, HG_DK, HG_DK), F32),
        pltpu.VMEM((tt, 2 * ML_WIDTH), BF16),
        pltpu.VMEM((tt, HG_WIDTH), F32),
        pltpu.VMEM((tt, HG_WIDTH), F32),
        pltpu.VMEM((tt, HG_WIDTH), F32),
        pltpu.VMEM((tt, HG_WIDTH), F32),
        pltpu.VMEM((tt, HG_WIDTH), F32),
    ]
    return pl.pallas_call(
        functools.partial(_prompt_mixer_kernel, tt=tt),
        grid=(bsz, seq // tt),
        in_specs=[pl.BlockSpec((1, tt, D_MODEL), lambda b, t: (b, t, 0))] + [_const_spec(c.shape) for c in consts],
        out_specs=out_specs,
        out_shape=out_shape,
        scratch_shapes=scratch,
        compiler_params=pltpu.CompilerParams(dimension_semantics=("arbitrary", "arbitrary"),
                                             vmem_limit_bytes=VMEM_LIMIT),
        name="prompt_mixer",
    )(x, *consts)


DEC_BLOCK = SUBLANES


def _decode_mixer_kernel(x_ref, conv_ref, c_ref, n_ref, m_ref, s_ref,
                         w_main_ref, w_gate_ref, b_main_ref, b_gate_ref, conv_w_ref, conv_b_ref,
                         ml_g_ref, hg_g_ref, log_lb_ref, log1m_lb_ref, one_m_lb_ref, w_out_ref,
                         ln_g_ref, ln_b_ref,
                         x1_ref, conv_out_ref, c_out_ref, n_out_ref, m_out_ref, s_out_ref,
                         q_scr, k_scr, v_scr, og_scr, li_scr, lf_scr, hq_scr, hk_scr, hf_scr, hi_scr, hgate_scr,
                         qt_scr, kt_scr, hqt_scr, hkt_scr, hft_scr, hml_scr, ho_scr, *, nb):
    i = pl.program_id(0)
    nsteps = pl.num_programs(0)

    @pl.when(i == 0)
    def _():
        xb = x_ref[...].astype(BF16)

        def proj(lo, hi):
            return _dot(xb, w_main_ref[:, lo:hi]) + b_main_ref[:, lo:hi]

        qk_raw = proj(0, 2 * ML_WIDTH)
        w = 2 * ML_WIDTH
        qk = conv_b_ref[...] + qk_raw * conv_w_ref[CONV_WIDTH - 1:CONV_WIDTH, :]
        for j in range(CONV_WIDTH - 1):
            qk = qk + conv_ref[:, j * w:(j + 1) * w] * conv_w_ref[j:j + 1, :]
        qk = _silu(qk)
        conv_out_ref[:, 0:2 * w] = conv_ref[:, w:3 * w]
        conv_out_ref[:, 2 * w:3 * w] = qk_raw
        q_scr[...] = qk[:, :ML_WIDTH]
        k_scr[...] = qk[:, ML_WIDTH:] * (ML_DK ** -0.5)
        v_scr[...] = proj(2 * ML_WIDTH, 3 * ML_WIDTH)
        og_scr[...] = _sigmoid(proj(3 * ML_WIDTH, 4 * ML_WIDTH))
        zg = _dot(xb, w_gate_ref[...]) + b_gate_ref[...]
        li_scr[...] = zg[:, 0:LANES]
        lf_scr[...] = _log_sigmoid(zg[:, LANES:2 * LANES])
        base = 4 * ML_WIDTH
        hq_scr[...] = _silu(proj(base, base + HG_WIDTH))
        lf_h, k_h = _hg_gates(proj(base + HG_WIDTH, base + 2 * HG_WIDTH),
                              log_lb_ref[...], log1m_lb_ref[...], one_m_lb_ref[...])
        hk_scr[...] = k_h
        hf_scr[...] = jnp.exp(lf_h)
        hi_scr[...] = proj(base + 2 * HG_WIDTH, base + 3 * HG_WIDTH)
        hgate_scr[...] = _silu(proj(base + 3 * HG_WIDTH, base + 4 * HG_WIDTH))
        for h in range(ML_HEADS):
            hs = slice(h * ML_DK, (h + 1) * ML_DK)
            qt_scr[h] = q_scr[:, hs].T
            kt_scr[h] = k_scr[:, hs].T
            hqt_scr[h] = hq_scr[:, hs].T
            hkt_scr[h] = hk_scr[:, hs].T
            hft_scr[h] = hf_scr[:, hs].T

    b0 = pl.multiple_of(i * DEC_BLOCK, DEC_BLOCK)
    shift = (nb - b0) & (nb - 1)
    rows = pl.ds(b0, DEC_BLOCK)
    li = li_scr[rows, :]
    lf = lf_scr[rows, :]
    m_prev = m_ref[...]
    q_blk = q_scr[rows, :]
    k_blk = k_scr[rows, :]
    v_blk = v_scr[rows, :]
    hi_blk = hi_scr[rows, :]
    n_blk = n_ref[...]
    m_new_cols = []
    for h in range(ML_HEADS):
        hs = slice(h * ML_DK, (h + 1) * ML_DK)
        li_h = li[:, h:h + 1]
        lf_h = lf[:, h:h + 1]
        mp_h = m_prev[:, h:h + 1]
        m_t = jnp.maximum(lf_h + mp_h, li_h)
        dmat = jnp.exp(li_h - m_t)
        dec = jnp.exp(lf_h + mp_h - m_t)
        n_h = n_blk[:, hs]
        s = jnp.sum(q_blk[:, hs] * k_blk[:, hs], axis=1, keepdims=True) * dmat
        qn = s + dec * jnp.sum(q_blk[:, hs] * n_h, axis=1, keepdims=True)
        denom = jnp.maximum(jnp.abs(qn), jnp.exp(-m_t))
        n_out_ref[:, hs] = dec * n_h + dmat * k_blk[:, hs]
        m_new_cols.append(m_t)
        qt = pltpu.roll(qt_scr[h], shift, 1)
        kt = pltpu.roll(kt_scr[h], shift, 1)
        hqt = pltpu.roll(hqt_scr[h], shift, 1)
        hkt = pltpu.roll(hkt_scr[h], shift, 1)
        hft = pltpu.roll(hft_scr[h], shift, 1)
        qc_rows, ho_rows = [], []
        for j in range(DEC_BLOCK):
            jr = slice(j, j + 1)
            cmat = c_ref[j, h]
            qc_rows.append(jnp.sum(qt[:, jr] * cmat, axis=0, keepdims=True))
            c_out_ref[j, h] = dec[jr, :] * cmat + (dmat[jr, :] * kt[:, jr]) * v_blk[jr, hs]
            s_new = hft[:, jr] * s_ref[j, h] + hkt[:, jr] * hi_blk[jr, hs]
            s_out_ref[j, h] = s_new
            ho_rows.append(jnp.sum(hqt[:, jr] * s_new, axis=0, keepdims=True))
        num = s * v_blk[:, hs] + dec * jnp.concatenate(qc_rows, axis=0)
        hml_scr[rows, hs] = num / denom
        ho_scr[rows, hs] = jnp.concatenate(ho_rows, axis=0)
    m_out_ref[...] = jnp.concatenate(m_new_cols, axis=1)

    @pl.when(i == nsteps - 1)
    def _():
        for h in range(ML_HEADS):
            hs = slice(h * ML_DK, (h + 1) * ML_DK)
            hm = _head_rms(og_scr[:, hs] * hml_scr[:, hs], ml_g_ref[:, hs])
            hml_scr[:, hs] = hm
            ho = _head_rms(ho_scr[:, hs], hg_g_ref[:, hs]) * hgate_scr[:, hs]
            ho_scr[:, hs] = ho
        mix = (_dot(hml_scr[...].astype(BF16), w_out_ref[0:ML_WIDTH, :])
               + _dot(ho_scr[...].astype(BF16), w_out_ref[ML_WIDTH:2 * ML_WIDTH, :]))
        x1_ref[...] = _layer_norm_rows(DEEPNORM_ALPHA * x_ref[...] + mix, ln_g_ref[...], ln_b_ref[...])


def _decode_mixer(x, conv, c_state, n_state, m_state, s_state, p):
    nb = x.shape[0]
    assert nb == LANES, "decode batch must fill one lane tile"
    consts = (p["w_main"], p["w_gate"], p["b_main"], p["b_gate"], p["conv_w"], p["conv_b"], p["ml_g"],
              p["hg_g"], p["log_lb"], p["log1m_lb"], p["one_m_lb"], p["w_out"], p["ln1_g"], p["ln1_b"])
    conv_cols = (CONV_WIDTH - 1) * 2 * ML_WIDTH
    blk4 = lambda i: (i, 0, 0, 0)
    state_specs = [
        pl.BlockSpec((DEC_BLOCK, ML_HEADS, ML_DK, ML_DK), blk4),
        pl.BlockSpec((DEC_BLOCK, ML_WIDTH), lambda i: (i, 0)),
        pl.BlockSpec((DEC_BLOCK, ML_HEADS), lambda i: (i, 0)),
        pl.BlockSpec((DEC_BLOCK, HG_HEADS, HG_DK, HG_DK), blk4),
    ]
    full2 = lambda shape: pl.BlockSpec(shape, lambda i: (0, 0))
    out_shape = (
        jax.ShapeDtypeStruct((nb, D_MODEL), F32),
        jax.ShapeDtypeStruct((nb, conv_cols), F32),
        jax.ShapeDtypeStruct(c_state.shape, F32),
        jax.ShapeDtypeStruct(n_state.shape, F32),
        jax.ShapeDtypeStruct(m_state.shape, F32),
        jax.ShapeDtypeStruct(s_state.shape, F32),
    )
    rows_f32 = lambda w: pltpu.VMEM((nb, w), F32)
    scratch = ([rows_f32(ML_WIDTH)] * 4 + [rows_f32(LANES)] * 2 + [rows_f32(HG_WIDTH)] * 5
               + [pltpu.VMEM((ML_HEADS, ML_DK, nb), F32)] * 5 + [rows_f32(ML_WIDTH), rows_f32(HG_WIDTH)])
    return pl.pallas_call(
        functools.partial(_decode_mixer_kernel, nb=nb),
        grid=(nb // DEC_BLOCK,),
        in_specs=[full2((nb, D_MODEL)), full2((nb, conv_cols))] + state_specs + [_const_spec(c.shape) for c in consts],
        out_specs=(full2((nb, D_MODEL)), full2((nb, conv_cols))) + tuple(state_specs),
        out_shape=out_shape,
        scratch_shapes=scratch,
        compiler_params=pltpu.CompilerParams(dimension_semantics=("arbitrary",), vmem_limit_bytes=VMEM_LIMIT),
        name="decode_mixer",
    )(x, conv, c_state, n_state, m_state, s_state, *consts)


def _moe_kernel(x_ref, w_r_ref, b_r_ref, w1_ref, w3_ref, w2_ref, ln_g_ref, ln_b_ref, y_ref, hdn_scr):
    x = x_ref[...]
    xb = x.astype(BF16)
    logits = _dot(xb, w_r_ref[...]) + b_r_ref[...]
    lg = logits[:, 0:N_GROUPS]
    le = logits[:, N_GROUPS:N_GROUPS + N_EXPERTS]
    gidx = lax.broadcasted_iota(jnp.int32, lg.shape, 1)
    gmax = jnp.max(lg, axis=1, keepdims=True)
    g = jnp.min(jnp.where(lg == gmax, gidx, N_GROUPS), axis=1, keepdims=True)
    pg = 1.0 / jnp.sum(jnp.exp(lg - gmax), axis=1, keepdims=True)
    eidx = lax.broadcasted_iota(jnp.int32, le.shape, 1)
    in_g = (eidx >> 2) == g
    lm = jnp.where(in_g, le, -jnp.inf)
    v1 = jnp.max(lm, axis=1, keepdims=True)
    i1 = jnp.min(jnp.where(lm == v1, eidx, N_EXPERTS), axis=1, keepdims=True)
    lm2 = jnp.where(eidx == i1, -jnp.inf, lm)
    v2 = jnp.max(lm2, axis=1, keepdims=True)
    i2 = jnp.min(jnp.where(lm2 == v2, eidx, N_EXPERTS), axis=1, keepdims=True)
    e21 = jnp.exp(v2 - v1)
    wt1 = pg / (1.0 + e21)
    wt2 = wt1 * e21
    gate = jnp.where(eidx == i1, wt1, 0.0) + jnp.where(eidx == i2, wt2, 0.0)
    for e in range(N_EXPERTS):
        a = _dot(xb, w1_ref[e])
        u = _dot(xb, w3_ref[e])
        hdn = _silu(a) * u * gate[:, e:e + 1]
        hdn_scr[:, e * EXPERT_FF:(e + 1) * EXPERT_FF] = hdn.astype(BF16)
    moe = _dot(hdn_scr[...], w2_ref[...])
    y_ref[...] = _layer_norm_rows(DEEPNORM_ALPHA * x + moe, ln_g_ref[...], ln_b_ref[...])


def _moe(x, p, tm):
    n = x.shape[0]
    consts = (p["w_router"], p["b_router"], p["w_gate_e"], p["w_up_e"], p["w_down_e"], p["ln2_g"], p["ln2_b"])
    return pl.pallas_call(
        _moe_kernel,
        grid=(n // tm,),
        in_specs=[pl.BlockSpec((tm, D_MODEL), lambda i: (i, 0))] + [_const_spec(c.shape) for c in consts],
        out_specs=pl.BlockSpec((tm, D_MODEL), lambda i: (i, 0)),
        out_shape=jax.ShapeDtypeStruct((n, D_MODEL), F32),
        scratch_shapes=[pltpu.VMEM((tm, N_EXPERTS * EXPERT_FF), BF16)],
        compiler_params=pltpu.CompilerParams(dimension_semantics=("arbitrary",), vmem_limit_bytes=VMEM_LIMIT),
        name="moe",
    )(x, *consts)


def _layer_params(l, lbs, w_in, b_in, conv_w, conv_b, ml_norm_g, hg_norm_g, w_out, ln1_g, ln1_b,
                  w_router_group, b_router_group, w_router_expert, b_router_expert,
                  w_gate, w_up, w_down, ln2_g, ln2_b):
    g0 = 4 * ML_WIDTH
    w = w_in[l]
    b = b_in[l]
    w_main = jnp.concatenate([w[:, :g0], w[:, ML_COLS:]], axis=1).astype(BF16)
    b_main = jnp.concatenate([b[:g0], b[ML_COLS:]])[None, :]
    zw = jnp.zeros((D_MODEL, LANES - ML_HEADS), F32)
    zb = jnp.zeros((LANES - ML_HEADS,), F32)
    w_gate_cols = jnp.concatenate([w[:, g0:g0 + ML_HEADS], zw, w[:, g0 + ML_HEADS:ML_COLS], zw], axis=1).astype(BF16)
    b_gate_cols = jnp.concatenate([b[g0:g0 + ML_HEADS], zb, b[g0 + ML_HEADS:ML_COLS], zb])[None, :]
    pad_r = LANES - N_GROUPS - N_EXPERTS
    w_router = jnp.concatenate([w_router_group[l], w_router_expert[l], jnp.zeros((D_MODEL, pad_r), F32)],
                               axis=1).astype(BF16)
    b_router = jnp.concatenate([b_router_group[l], b_router_expert[l], jnp.zeros((pad_r,), F32)])[None, :]
    lb = lbs[l]
    return {
        "w_main": w_main, "b_main": b_main, "w_gate": w_gate_cols, "b_gate": b_gate_cols,
        "conv_w": conv_w[l], "conv_b": conv_b[l][None, :], "ml_g": ml_norm_g[l][None, :],
        "hg_g": hg_norm_g[l][None, :], "log_lb": jnp.log(lb)[None, :], "log1m_lb": jnp.log1p(-lb)[None, :],
        "one_m_lb": (1.0 - lb)[None, :], "w_out": w_out[l].astype(BF16),
        "ln1_g": ln1_g[l][None, :], "ln1_b": ln1_b[l][None, :],
        "w_router": w_router, "b_router": b_router,
        "w_gate_e": w_gate[l].astype(BF16), "w_up_e": w_up[l].astype(BF16),
        "w_down_e": w_down[l].astype(BF16).reshape(N_EXPERTS * EXPERT_FF, D_MODEL),
        "ln2_g": ln2_g[l][None, :], "ln2_b": ln2_b[l][None, :],
    }


def _pick_tile(n, pref):
    t = pref
    while n % t:
        t //= 2
    return t


def kernel(x_prompt, x_sample, state_conv, state_ml_C, state_ml_n, state_ml_m, state_hg_S, w_in, b_in, conv_w, conv_b, ml_norm_g, hg_lower_bounds, hg_norm_g, w_out, ln1_g, ln1_b, w_router_group, b_router_group, w_router_expert, b_router_expert, w_gate, w_up, w_down, ln2_g, ln2_b):
    sm = jax.nn.softmax(hg_lower_bounds.astype(F32), axis=0)
    lbs = jnp.concatenate([jnp.zeros_like(sm[:1]), jnp.cumsum(sm[1:], axis=0)], axis=0)
    bsz, seq, _ = x_prompt.shape
    nb = x_sample.shape[0]
    tt = _pick_tile(seq, 256)
    y_p = x_prompt
    y_s = x_sample.reshape(nb, D_MODEL)
    st_p, st_s = [], []
    for l in range(DEPTH):
        p = _layer_params(l, lbs, w_in, b_in, conv_w, conv_b, ml_norm_g, hg_norm_g, w_out, ln1_g, ln1_b,
                          w_router_group, b_router_group, w_router_expert, b_router_expert,
                          w_gate, w_up, w_down, ln2_g, ln2_b)
        x1, conv_p, caug, m_p, s_p = _prompt_mixer(y_p, p, tt)
        n_tok = bsz * seq
        y_p = _moe(x1.reshape(n_tok, D_MODEL), p, _pick_tile(n_tok, 256)).reshape(bsz, seq, D_MODEL)
        st_p.append((conv_p, caug[..., :ML_DK], caug[..., ML_DK], m_p[:, :ML_HEADS, 0], s_p))
        x1s, conv_s, c_s, n_s, m_s, s_s = _decode_mixer(
            y_s, state_conv[l].reshape(nb, -1), state_ml_C[l], state_ml_n[l].reshape(nb, ML_WIDTH),
            state_ml_m[l], state_hg_S[l], p)
        y_s = _moe(x1s, p, _pick_tile(nb, 256))
        st_s.append((conv_s.reshape(nb, CONV_WIDTH - 1, 2 * ML_WIDTH), c_s,
                     n_s.reshape(nb, ML_HEADS, ML_DK), m_s, s_s))

    def stack(states, idx):
        return jnp.stack([s[idx] for s in states])

    return ((y_p, y_s.reshape(nb, 1, D_MODEL))
            + tuple(stack(st_p, i) for i in range(5)) + tuple(stack(st_s, i) for i in range(5)))
```

```python
import functools

import jax
import jax.numpy as jnp
from jax import lax
from jax.experimental import pallas as pl
from jax.experimental.pallas import tpu as pltpu

F32 = jnp.float32
BF16 = jnp.bfloat16

D_MODEL = 1024
DEPTH = 2
ML_HEADS = 4
ML_WIDTH = 512
ML_DK = 128
ML_CHUNK = 128
CONV_WIDTH = 4
HG_HEADS = 4
HG_WIDTH = 512
HG_DK = 128
HG_CHUNK = 64
ML_COLS = 4 * ML_WIDTH + 2 * ML_HEADS
N_GROUPS = 4
EXPERTS_PER_GROUP = 4
N_EXPERTS = 16
EXPERT_FF = 256
DEEPNORM_ALPHA = (2 * DEPTH) ** 0.25
LN_EPS = 1e-5
RMS_EPS = 1e-6

LANES = 128
SUBLANES = 8
MAIN_COLS = 4 * ML_WIDTH + 4 * HG_WIDTH
GATE_COLS = 2 * LANES
CONV_PAD = SUBLANES
HG_SAFE_DECAY = 150.0
VMEM_LIMIT = 56 * 1024 * 1024


def _dot(a, b):
    return jnp.dot(a, b, preferred_element_type=F32)


def _dot_nt(a, b):
    return lax.dot_general(a, b, (((1,), (1,)), ((), ())), preferred_element_type=F32)


def _dot_tn(a, b):
    return lax.dot_general(a, b, (((0,), (0,)), ((), ())), preferred_element_type=F32)


def _sigmoid(x):
    return 1.0 / (1.0 + jnp.exp(-x))


def _silu(x):
    return x * _sigmoid(x)


def _log_sigmoid(x):
    return jnp.minimum(x, 0.0) - jnp.log(1.0 + jnp.exp(-jnp.abs(x)))


def _cumsum_rows(tri_bf, x):
    hi = x.astype(BF16)
    r1 = x - hi.astype(F32)
    mid = r1.astype(BF16)
    lo = (r1 - mid.astype(F32)).astype(BF16)
    return _dot(tri_bf, hi) + _dot(tri_bf, mid) + _dot(tri_bf, lo)


def _layer_norm_rows(x, g, b):
    mu = jnp.mean(x, axis=-1, keepdims=True)
    xc = x - mu
    var = jnp.mean(xc * xc, axis=-1, keepdims=True)
    return xc * lax.rsqrt(var + LN_EPS) * g + b


def _head_rms(h, g):
    return h * lax.rsqrt(jnp.mean(h * h, axis=-1, keepdims=True) + RMS_EPS) * g


def _hg_gates(hf, log_lb, log1m_lb, one_m_lb):
    e = jnp.exp(-jnp.abs(hf))
    one_p_e = 1.0 + e
    log_sig = jnp.minimum(hf, 0.0) - jnp.log(one_p_e)
    b = log1m_lb + log_sig
    hi = jnp.maximum(log_lb, b)
    lo = jnp.minimum(log_lb, b)
    lf = hi + jnp.log(1.0 + jnp.exp(lo - hi))
    k = one_m_lb * jnp.where(hf >= 0.0, e, 1.0) / one_p_e
    return lf, k


def _prompt_mixer_kernel(x_ref, w_main_ref, w_gate_ref, b_main_ref, b_gate_ref, conv_w_ref, conv_b_ref,
                         ml_g_ref, hg_g_ref, log_lb_ref, log1m_lb_ref, one_m_lb_ref, w_out_ref,
                         ln_g_ref, ln_b_ref,
                         x1_ref, conv_out_ref, caug_out_ref, m_out_ref, s_out_ref,
                         xp_scr, caug_scr, m_scr, st_scr, mixed_scr, hq_scr, hk_scr, hi_scr, hgc_scr, ho_scr,
                         hinter_scr, *, tt):
    t = pl.program_id(1)
    nt = pl.num_programs(1)

    @pl.when(t == 0)
    def _():
        xp_scr[0:CONV_PAD, :] = jnp.zeros((CONV_PAD, 2 * ML_WIDTH), F32)
        caug_scr[...] = jnp.zeros_like(caug_scr)
        m_scr[...] = jnp.zeros_like(m_scr)
        st_scr[...] = jnp.zeros_like(st_scr)

    x = x_ref[0]
    xb = x.astype(BF16)

    def proj(lo, hi):
        return _dot(xb, w_main_ref[:, lo:hi]) + b_main_ref[:, lo:hi]

    xp_scr[CONV_PAD:CONV_PAD + tt, :] = proj(0, 2 * ML_WIDTH)
    qk = conv_b_ref[...] + xp_scr[CONV_PAD - 3:CONV_PAD - 3 + tt, :] * conv_w_ref[0:1, :]
    for j in range(1, CONV_WIDTH):
        qk = qk + xp_scr[CONV_PAD - 3 + j:CONV_PAD - 3 + j + tt, :] * conv_w_ref[j:j + 1, :]
    qk = _silu(qk)
    last_rows = xp_scr[CONV_PAD + tt - 3:CONV_PAD + tt, :]
    xp_scr[CONV_PAD - 3:CONV_PAD, :] = last_rows

    q_all = qk[:, :ML_WIDTH].astype(BF16)
    k_all = qk[:, ML_WIDTH:] * (ML_DK ** -0.5)
    v_all = proj(2 * ML_WIDTH, 3 * ML_WIDTH).astype(BF16)
    o_gate = _sigmoid(proj(3 * ML_WIDTH, 4 * ML_WIDTH))
    zg = _dot(xb, w_gate_ref[...]) + b_gate_ref[...]

    row = lax.broadcasted_iota(jnp.int32, (ML_CHUNK, ML_CHUNK), 0)
    col = lax.broadcasted_iota(jnp.int32, (ML_CHUNK, ML_CHUNK), 1)
    tril = row >= col
    tri_bf = jnp.where(tril, 1.0, 0.0).astype(BF16)
    ones_col = jnp.where(lax.broadcasted_iota(jnp.int32, (ML_CHUNK, LANES), 1) == 0, 1.0, 0.0).astype(BF16)

    for c in range(tt // ML_CHUNK):
        r0 = c * ML_CHUNK
        li_s = zg[r0:r0 + ML_CHUNK, 0:LANES]
        lf_s = _log_sigmoid(zg[r0:r0 + ML_CHUNK, LANES:2 * LANES])
        f_s = _cumsum_rows(tri_bf, lf_s)
        a_s = li_s - f_s
        a_t = a_s.T
        for h in range(ML_HEADS):
            hs = slice(h * ML_DK, (h + 1) * ML_DK)
            q = q_all[r0:r0 + ML_CHUNK, hs]
            k = k_all[r0:r0 + ML_CHUNK, hs]
            vaug = jnp.concatenate([v_all[r0:r0 + ML_CHUNK, hs], ones_col], axis=1)
            f_c = f_s[:, h:h + 1]
            a_c = a_s[:, h:h + 1]
            a_r = a_t[h:h + 1, :]
            m_prev = m_scr[h:h + 1, 0:1]
            cmax = jnp.max(jnp.where(tril, a_r, -jnp.inf), axis=1, keepdims=True)
            m_t = f_c + jnp.maximum(m_prev, cmax)
            dmat = jnp.where(tril, jnp.exp((f_c - m_t) + a_r), 0.0)
            s = _dot_nt(q, k.astype(BF16)) * dmat
            dec = jnp.exp(f_c + m_prev - m_t)
            caug = caug_scr[h]
            tot = _dot(s.astype(BF16), vaug) + dec * _dot(q, caug.astype(BF16))
            num = tot[:, :ML_DK]
            qn = tot[:, ML_DK:ML_DK + 1]
            hout = num / jnp.maximum(jnp.abs(qn), jnp.exp(-m_t))
            f_end = f_c[ML_CHUNK - 1:ML_CHUNK, :]
            m_end = m_t[ML_CHUNK - 1:ML_CHUNK, :]
            w_end = jnp.exp(f_end + a_c - m_end)
            dec_end = jnp.exp(f_end + m_prev - m_end)
            caug_scr[h] = dec_end * caug + _dot_tn((k * w_end).astype(BF16), vaug)
            m_scr[h:h + 1, :] = jnp.broadcast_to(m_end, (1, LANES))
            hout = _head_rms(o_gate[r0:r0 + ML_CHUNK, hs] * hout, ml_g_ref[:, hs])
            mixed_scr[r0:r0 + ML_CHUNK, hs] = hout.astype(BF16)

    base = 4 * ML_WIDTH
    hq_scr[...] = _silu(proj(base, base + HG_WIDTH))
    lf_h, k_h = _hg_gates(proj(base + HG_WIDTH, base + 2 * HG_WIDTH),
                          log_lb_ref[...], log1m_lb_ref[...], one_m_lb_ref[...])
    hk_scr[...] = k_h
    hi_scr[...] = proj(base + 2 * HG_WIDTH, base + 3 * HG_WIDTH)
    rt = lax.broadcasted_iota(jnp.int32, (tt, tt), 0)
    ct = lax.broadcasted_iota(jnp.int32, (tt, tt), 1)
    blk_tri = jnp.where((rt >= ct) & ((rt >> 6) == (ct >> 6)), 1.0, 0.0).astype(BF16)
    g_all = _cumsum_rows(blk_tri, lf_h)
    hgc_scr[...] = g_all
    tril64 = tril[:HG_CHUNK, :HG_CHUNK]
    row64 = row[:HG_CHUNK, :1]

    g_min = None
    for c in range(tt // HG_CHUNK):
        r0 = c * HG_CHUNK
        rows = slice(r0, r0 + HG_CHUNK)
        for h in range(HG_HEADS):
            hs = slice(h * HG_DK, (h + 1) * HG_DK)
            g = g_all[rows, hs]
            q = hq_scr[rows, hs]
            k = k_h[rows, hs]
            iv = hi_scr[rows, hs].astype(BF16)
            g_end = g[HG_CHUNK - 1:HG_CHUNK, :]
            g_min = g_end if g_min is None else jnp.minimum(g_min, g_end)
            st = st_scr[h]
            inter = _dot_nt((q * jnp.exp(g)).astype(BF16), st.astype(BF16))
            ref = 0.5 * g_end
            qa = (q * jnp.exp(g - ref)).astype(BF16)
            kb = (k * jnp.exp(ref - g)).astype(BF16)
            amat = jnp.where(tril64, _dot_nt(qa, kb), 0.0)
            hinter_scr[rows, hs] = inter
            ho_scr[rows, hs] = _dot(amat.astype(BF16), iv) + inter
            st_scr[h] = jnp.exp(g_end) * st + _dot_tn(iv, (k * jnp.exp(g_end - g)).astype(BF16))

    @pl.when(jnp.min(g_min) <= -HG_SAFE_DECAY)
    def _():
        for c in range(tt // HG_CHUNK):
            r0 = c * HG_CHUNK
            rows = slice(r0, r0 + HG_CHUNK)
            for h in range(HG_HEADS):
                hs = slice(h * HG_DK, (h + 1) * HG_DK)
                g = hgc_scr[rows, hs]
                k = hk_scr[rows, hs]
                ivf = hi_scr[rows, hs]

                def body(grp, carry):
                    rows8 = pl.ds(pl.multiple_of(r0 + grp * SUBLANES, SUBLANES), SUBLANES)
                    g8 = hgc_scr[rows8, hs]
                    q8 = hq_scr[rows8, hs]
                    outs = []
                    for jj in range(SUBLANES):
                        w = jnp.exp(jnp.minimum(g8[jj:jj + 1, :] - g, 0.0)) * k * q8[jj:jj + 1, :]
                        a_col = jnp.sum(w, axis=1, keepdims=True)
                        a_col = jnp.where(row64 <= grp * SUBLANES + jj, a_col, 0.0)
                        outs.append(jnp.sum(a_col * ivf, axis=0, keepdims=True))
                    ho_scr[rows8, hs] = jnp.concatenate(outs, axis=0) + hinter_scr[rows8, hs]
                    return carry

                lax.fori_loop(0, HG_CHUNK // SUBLANES, body, 0)

    hg_gate = _silu(proj(base + 3 * HG_WIDTH, base + 4 * HG_WIDTH))
    for h in range(HG_HEADS):
        hs = slice(h * HG_DK, (h + 1) * HG_DK)
        o = _head_rms(ho_scr[:, hs], hg_g_ref[:, hs]) * hg_gate[:, hs]
        mixed_scr[:, ML_WIDTH + h * HG_DK:ML_WIDTH + (h + 1) * HG_DK] = o.astype(BF16)

    mix = _dot(mixed_scr[...], w_out_ref[...])
    x1_ref[0] = _layer_norm_rows(DEEPNORM_ALPHA * x + mix, ln_g_ref[...], ln_b_ref[...])

    @pl.when(t == nt - 1)
    def _():
        for h in range(ML_HEADS):
            caug_out_ref[0, h] = caug_scr[h]
            s_out_ref[0, h] = st_scr[h].T
        m_out_ref[0] = m_scr[...]
        conv_out_ref[0] = xp_scr[CONV_PAD - 3:CONV_PAD, :]


def _const_spec(shape):
    nd = len(shape)
    return pl.BlockSpec(shape, lambda *_: (0,) * nd, pipeline_mode=pl.Buffered(1))


def _prompt_mixer(x, p, tt):
    bsz, seq, _ = x.shape
    consts = (p["w_main"], p["w_gate"], p["b_main"], p["b_gate"], p["conv_w"], p["conv_b"], p["ml_g"],
              p["hg_g"], p["log_lb"], p["log1m_lb"], p["one_m_lb"], p["w_out"], p["ln1_g"], p["ln1_b"])
    out_shape = (
        jax.ShapeDtypeStruct((bsz, seq, D_MODEL), F32),
        jax.ShapeDtypeStruct((bsz, CONV_WIDTH - 1, 2 * ML_WIDTH), F32),
        jax.ShapeDtypeStruct((bsz, ML_HEADS, ML_DK, 2 * ML_DK), F32),
        jax.ShapeDtypeStruct((bsz, SUBLANES, LANES), F32),
        jax.ShapeDtypeStruct((bsz, HG_HEADS, HG_DK, HG_DK), F32),
    )
    out_specs = (
        pl.BlockSpec((1, tt, D_MODEL), lambda b, t: (b, t, 0)),
        pl.BlockSpec((1, CONV_WIDTH - 1, 2 * ML_WIDTH), lambda b, t: (b, 0, 0)),
        pl.BlockSpec((1, ML_HEADS, ML_DK, 2 * ML_DK), lambda b, t: (b, 0, 0, 0)),
        pl.BlockSpec((1, SUBLANES, LANES), lambda b, t: (b, 0, 0)),
        pl.BlockSpec((1, HG_HEADS, HG_DK, HG_DK), lambda b, t: (b, 0, 0, 0)),
    )
    scratch = [
        pltpu.VMEM((CONV_PAD + tt, 2 * ML_WIDTH), F32),
        pltpu.VMEM((ML_HEADS, ML_DK, 2 * ML_DK), F32),
        pltpu.VMEM((SUBLANES, LANES), F32),
        pltpu.VMEM((HG_HEADS, HG_DK, HG_DK), F32),
        pltpu.VMEM((tt, 2 * ML_WIDTH), BF16),
        pltpu.VMEM((tt, HG_WIDTH), F32),
        pltpu.VMEM((tt, HG_WIDTH), F32),
        pltpu.VMEM((tt, HG_WIDTH), F32),
        pltpu.VMEM((tt, HG_WIDTH), F32),
        pltpu.VMEM((tt, HG_WIDTH), F32),
        pltpu.VMEM((tt, HG_WIDTH), F32),
    ]
    return pl.pallas_call(
        functools.partial(_prompt_mixer_kernel, tt=tt),
        grid=(bsz, seq // tt),
        in_specs=[pl.BlockSpec((1, tt, D_MODEL), lambda b, t: (b, t, 0))] + [_const_spec(c.shape) for c in consts],
        out_specs=out_specs,
        out_shape=out_shape,
        scratch_shapes=scratch,
        compiler_params=pltpu.CompilerParams(dimension_semantics=("arbitrary", "arbitrary"),
                                             vmem_limit_bytes=VMEM_LIMIT),
        name="prompt_mixer",
    )(x, *consts)


DEC_BLOCK = SUBLANES


def _decode_mixer_kernel(x_ref, conv_ref, c_ref, n_ref, m_ref, s_ref,
                         w_main_ref, w_gate_ref, b_main_ref, b_gate_ref, conv_w_ref, conv_b_ref,
                         ml_g_ref, hg_g_ref, log_lb_ref, log1m_lb_ref, one_m_lb_ref, w_out_ref,
                         ln_g_ref, ln_b_ref,
                         x1_ref, conv_out_ref, c_out_ref, n_out_ref, m_out_ref, s_out_ref,
                         q_scr, k_scr, v_scr, og_scr, li_scr, lf_scr, hq_scr, hk_scr, hf_scr, hi_scr, hgate_scr,
                         qt_scr, kt_scr, hqt_scr, hkt_scr, hft_scr, hml_scr, ho_scr, *, nb):
    i = pl.program_id(0)
    nsteps = pl.num_programs(0)

    @pl.when(i == 0)
    def _():
        xb = x_ref[...].astype(BF16)

        def proj(lo, hi):
            return _dot(xb, w_main_ref[:, lo:hi]) + b_main_ref[:, lo:hi]

        qk_raw = proj(0, 2 * ML_WIDTH)
        w = 2 * ML_WIDTH
        qk = conv_b_ref[...] + qk_raw * conv_w_ref[CONV_WIDTH - 1:CONV_WIDTH, :]
        for j in range(CONV_WIDTH - 1):
            qk = qk + conv_ref[:, j * w:(j + 1) * w] * conv_w_ref[j:j + 1, :]
        qk = _silu(qk)
        conv_out_ref[:, 0:2 * w] = conv_ref[:, w:3 * w]
        conv_out_ref[:, 2 * w:3 * w] = qk_raw
        q_scr[...] = qk[:, :ML_WIDTH]
        k_scr[...] = qk[:, ML_WIDTH:] * (ML_DK ** -0.5)
        v_scr[...] = proj(2 * ML_WIDTH, 3 * ML_WIDTH)
        og_scr[...] = _sigmoid(proj(3 * ML_WIDTH, 4 * ML_WIDTH))
        zg = _dot(xb, w_gate_ref[...]) + b_gate_ref[...]
        li_scr[...] = zg[:, 0:LANES]
        lf_scr[...] = _log_sigmoid(zg[:, LANES:2 * LANES])
        base = 4 * ML_WIDTH
        hq_scr[...] = _silu(proj(base, base + HG_WIDTH))
        lf_h, k_h = _hg_gates(proj(base + HG_WIDTH, base + 2 * HG_WIDTH),
                              log_lb_ref[...], log1m_lb_ref[...], one_m_lb_ref[...])
        hk_scr[...] = k_h
        hf_scr[...] = jnp.exp(lf_h)
        hi_scr[...] = proj(base + 2 * HG_WIDTH, base + 3 * HG_WIDTH)
        hgate_scr[...] = _silu(proj(base + 3 * HG_WIDTH, base + 4 * HG_WIDTH))
        for h in range(ML_HEADS):
            hs = slice(h * ML_DK, (h + 1) * ML_DK)
            qt_scr[h] = q_scr[:, hs].T
            kt_scr[h] = k_scr[:, hs].T
            hqt_scr[h] = hq_scr[:, hs].T
            hkt_scr[h] = hk_scr[:, hs].T
            hft_scr[h] = hf_scr[:, hs].T

    b0 = pl.multiple_of(i * DEC_BLOCK, DEC_BLOCK)
    shift = (nb - b0) & (nb - 1)
    rows = pl.ds(b0, DEC_BLOCK)
    li = li_scr[rows, :]
    lf = lf_scr[rows, :]
    m_prev = m_ref[...]
    q_blk = q_scr[rows, :]
    k_blk = k_scr[rows, :]
    v_blk = v_scr[rows, :]
    hi_blk = hi_scr[rows, :]
    n_blk = n_ref[...]
    m_new_cols = []
    for h in range(ML_HEADS):
        hs = slice(h * ML_DK, (h + 1) * ML_DK)
        li_h = li[:, h:h + 1]
        lf_h = lf[:, h:h + 1]
        mp_h = m_prev[:, h:h + 1]
        m_t = jnp.maximum(lf_h + mp_h, li_h)
        dmat = jnp.exp(li_h - m_t)
        dec = jnp.exp(lf_h + mp_h - m_t)
        n_h = n_blk[:, hs]
        s = jnp.sum(q_blk[:, hs] * k_blk[:, hs], axis=1, keepdims=True) * dmat
        qn = s + dec * jnp.sum(q_blk[:, hs] * n_h, axis=1, keepdims=True)
        denom = jnp.maximum(jnp.abs(qn), jnp.exp(-m_t))
        n_out_ref[:, hs] = dec * n_h + dmat * k_blk[:, hs]
        m_new_cols.append(m_t)
        qt = pltpu.roll(qt_scr[h], shift, 1)
        kt = pltpu.roll(kt_scr[h], shift, 1)
        hqt = pltpu.roll(hqt_scr[h], shift, 1)
        hkt = pltpu.roll(hkt_scr[h], shift, 1)
        hft = pltpu.roll(hft_scr[h], shift, 1)
        qc_rows, ho_rows = [], []
        for j in range(DEC_BLOCK):
            jr = slice(j, j + 1)
            cmat = c_ref[j, h]
            qc_rows.append(jnp.sum(qt[:, jr] * cmat, axis=0, keepdims=True))
            c_out_ref[j, h] = dec[jr, :] * cmat + (dmat[jr, :] * kt[:, jr]) * v_blk[jr, hs]
            s_new = hft[:, jr] * s_ref[j, h] + hkt[:, jr] * hi_blk[jr, hs]
            s_out_ref[j, h] = s_new
            ho_rows.append(jnp.sum(hqt[:, jr] * s_new, axis=0, keepdims=True))
        num = s * v_blk[:, hs] + dec * jnp.concatenate(qc_rows, axis=0)
        hml_scr[rows, hs] = num / denom
        ho_scr[rows, hs] = jnp.concatenate(ho_rows, axis=0)
    m_out_ref[...] = jnp.concatenate(m_new_cols, axis=1)

    @pl.when(i == nsteps - 1)
    def _():
        for h in range(ML_HEADS):
            hs = slice(h * ML_DK, (h + 1) * ML_DK)
            hm = _head_rms(og_scr[:, hs] * hml_scr[:, hs], ml_g_ref[:, hs])
            hml_scr[:, hs] = hm
            ho = _head_rms(ho_scr[:, hs], hg_g_ref[:, hs]) * hgate_scr[:, hs]
            ho_scr[:, hs] = ho
        mix = (_dot(hml_scr[...].astype(BF16), w_out_ref[0:ML_WIDTH, :])
               + _dot(ho_scr[...].astype(BF16), w_out_ref[ML_WIDTH:2 * ML_WIDTH, :]))
        x1_ref[...] = _layer_norm_rows(DEEPNORM_ALPHA * x_ref[...] + mix, ln_g_ref[...], ln_b_ref[...])


def _decode_mixer(x, conv, c_state, n_state, m_state, s_state, p):
    nb = x.shape[0]
    assert nb == LANES, "decode batch must fill one lane tile"
    consts = (p["w_main"], p["w_gate"], p["b_main"], p["b_gate"], p["conv_w"], p["conv_b"], p["ml_g"],
              p["hg_g"], p["log_lb"], p["log1m_lb"], p["one_m_lb"], p["w_out"], p["ln1_g"], p["ln1_b"])
    conv_cols = (CONV_WIDTH - 1) * 2 * ML_WIDTH
    blk4 = lambda i: (i, 0, 0, 0)
    state_specs = [
        pl.BlockSpec((DEC_BLOCK, ML_HEADS, ML_DK, ML_DK), blk4),
        pl.BlockSpec((DEC_BLOCK, ML_WIDTH), lambda i: (i, 0)),
        pl.BlockSpec((DEC_BLOCK, ML_HEADS), lambda i: (i, 0)),
        pl.BlockSpec((DEC_BLOCK, HG_HEADS, HG_DK, HG_DK), blk4),
    ]
    full2 = lambda shape: pl.BlockSpec(shape, lambda i: (0, 0))
    out_shape = (
        jax.ShapeDtypeStruct((nb, D_MODEL), F32),
        jax.ShapeDtypeStruct((nb, conv_cols), F32),
        jax.ShapeDtypeStruct(c_state.shape, F32),
        jax.ShapeDtypeStruct(n_state.shape, F32),
        jax.ShapeDtypeStruct(m_state.shape, F32),
        jax.ShapeDtypeStruct(s_state.shape, F32),
    )
    rows_f32 = lambda w: pltpu.VMEM((nb, w), F32)
    scratch = ([rows_f32(ML_WIDTH)] * 4 + [rows_f32(LANES)] * 2 + [rows_f32(HG_WIDTH)] * 5
               + [pltpu.VMEM((ML_HEADS, ML_DK, nb), F32)] * 5 + [rows_f32(ML_WIDTH), rows_f32(HG_WIDTH)])
    return pl.pallas_call(
        functools.partial(_decode_mixer_kernel, nb=nb),
        grid=(nb // DEC_BLOCK,),
        in_specs=[full2((nb, D_MODEL)), full2((nb, conv_cols))] + state_specs + [_const_spec(c.shape) for c in consts],
        out_specs=(full2((nb, D_MODEL)), full2((nb, conv_cols))) + tuple(state_specs),
        out_shape=out_shape,
        scratch_shapes=scratch,
        compiler_params=pltpu.CompilerParams(dimension_semantics=("arbitrary",), vmem_limit_bytes=VMEM_LIMIT),
        name="decode_mixer",
    )(x, conv, c_state, n_state, m_state, s_state, *consts)


def _moe_kernel(x_ref, w_r_ref, b_r_ref, w1_ref, w3_ref, w2_ref, ln_g_ref, ln_b_ref, y_ref, hdn_scr):
    x = x_ref[...]
    xb = x.astype(BF16)
    logits = _dot(xb, w_r_ref[...]) + b_r_ref[...]
    lg = logits[:, 0:N_GROUPS]
    le = logits[:, N_GROUPS:N_GROUPS + N_EXPERTS]
    gidx = lax.broadcasted_iota(jnp.int32, lg.shape, 1)
    gmax = jnp.max(lg, axis=1, keepdims=True)
    g = jnp.min(jnp.where(lg == gmax, gidx, N_GROUPS), axis=1, keepdims=True)
    pg = 1.0 / jnp.sum(jnp.exp(lg - gmax), axis=1, keepdims=True)
    eidx = lax.broadcasted_iota(jnp.int32, le.shape, 1)
    in_g = (eidx >> 2) == g
    lm = jnp.where(in_g, le, -jnp.inf)
    v1 = jnp.max(lm, axis=1, keepdims=True)
    i1 = jnp.min(jnp.where(lm == v1, eidx, N_EXPERTS), axis=1, keepdims=True)
    lm2 = jnp.where(eidx == i1, -jnp.inf, lm)
    v2 = jnp.max(lm2, axis=1, keepdims=True)
    i2 = jnp.min(jnp.where(lm2 == v2, eidx, N_EXPERTS), axis=1, keepdims=True)
    e21 = jnp.exp(v2 - v1)
    wt1 = pg / (1.0 + e21)
    wt2 = wt1 * e21
    gate = jnp.where(eidx == i1, wt1, 0.0) + jnp.where(eidx == i2, wt2, 0.0)
    for e in range(N_EXPERTS):
        a = _dot(xb, w1_ref[e])
        u = _dot(xb, w3_ref[e])
        hdn = _silu(a) * u * gate[:, e:e + 1]
        hdn_scr[:, e * EXPERT_FF:(e + 1) * EXPERT_FF] = hdn.astype(BF16)
    moe = _dot(hdn_scr[...], w2_ref[...])
    y_ref[...] = _layer_norm_rows(DEEPNORM_ALPHA * x + moe, ln_g_ref[...], ln_b_ref[...])


def _moe(x, p, tm):
    n = x.shape[0]
    consts = (p["w_router"], p["b_router"], p["w_gate_e"], p["w_up_e"], p["w_down_e"], p["ln2_g"], p["ln2_b"])
    return pl.pallas_call(
        _moe_kernel,
        grid=(n // tm,),
        in_specs=[pl.BlockSpec((tm, D_MODEL), lambda i: (i, 0))] + [_const_spec(c.shape) for c in consts],
        out_specs=pl.BlockSpec((tm, D_MODEL), lambda i: (i, 0)),
        out_shape=jax.ShapeDtypeStruct((n, D_MODEL), F32),
        scratch_shapes=[pltpu.VMEM((tm, N_EXPERTS * EXPERT_FF), BF16)],
        compiler_params=pltpu.CompilerParams(dimension_semantics=("arbitrary",), vmem_limit_bytes=VMEM_LIMIT),
        name="moe",
    )(x, *consts)


def _layer_params(l, lbs, w_in, b_in, conv_w, conv_b, ml_norm_g, hg_norm_g, w_out, ln1_g, ln1_b,
                  w_router_group, b_router_group, w_router_expert, b_router_expert,
                  w_gate, w_up, w_down, ln2_g, ln2_b):
    g0 = 4 * ML_WIDTH
    w = w_in[l]
    b = b_in[l]
    w_main = jnp.concatenate([w[:, :g0], w[:, ML_COLS:]], axis=1).astype(BF16)
    b_main = jnp.concatenate([b[:g0], b[ML_COLS:]])[None, :]
    zw = jnp.zeros((D_MODEL, LANES - ML_HEADS), F32)
    zb = jnp.zeros((LANES - ML_HEADS,), F32)
    w_gate_cols = jnp.concatenate([w[:, g0:g0 + ML_HEADS], zw, w[:, g0 + ML_HEADS:ML_COLS], zw], axis=1).astype(BF16)
    b_gate_cols = jnp.concatenate([b[g0:g0 + ML_HEADS], zb, b[g0 + ML_HEADS:ML_COLS], zb])[None, :]
    pad_r = LANES - N_GROUPS - N_EXPERTS
    w_router = jnp.concatenate([w_router_group[l], w_router_expert[l], jnp.zeros((D_MODEL, pad_r), F32)],
                               axis=1).astype(BF16)
    b_router = jnp.concatenate([b_router_group[l], b_router_expert[l], jnp.zeros((pad_r,), F32)])[None, :]
    lb = lbs[l]
    return {
        "w_main": w_main, "b_main": b_main, "w_gate": w_gate_cols, "b_gate": b_gate_cols,
        "conv_w": conv_w[l], "conv_b": conv_b[l][None, :], "ml_g": ml_norm_g[l][None, :],
        "hg_g": hg_norm_g[l][None, :], "log_lb": jnp.log(lb)[None, :], "log1m_lb": jnp.log1p(-lb)[None, :],
        "one_m_lb": (1.0 - lb)[None, :], "w_out": w_out[l].astype(BF16),
        "ln1_g": ln1_g[l][None, :], "ln1_b": ln1_b[l][None, :],
        "w_router": w_router, "b_router": b_router,
        "w_gate_e": w_gate[l].astype(BF16), "w_up_e": w_up[l].astype(BF16),
        "w_down_e": w_down[l].astype(BF16).reshape(N_EXPERTS * EXPERT_FF, D_MODEL),
        "ln2_g": ln2_g[l][None, :], "ln2_b": ln2_b[l][None, :],
    }


def _pick_tile(n, pref):
    t = pref
    while n % t:
        t //= 2
    return t


def kernel(x_prompt, x_sample, state_conv, state_ml_C, state_ml_n, state_ml_m, state_hg_S, w_in, b_in, conv_w, conv_b, ml_norm_g, hg_lower_bounds, hg_norm_g, w_out, ln1_g, ln1_b, w_router_group, b_router_group, w_router_expert, b_router_expert, w_gate, w_up, w_down, ln2_g, ln2_b):
    sm = jax.nn.softmax(hg_lower_bounds.astype(F32), axis=0)
    lbs = jnp.concatenate([jnp.zeros_like(sm[:1]), jnp.cumsum(sm[1:], axis=0)], axis=0)
    bsz, seq, _ = x_prompt.shape
    nb = x_sample.shape[0]
    tt = _pick_tile(seq, 256)
    y_p = x_prompt
    y_s = x_sample.reshape(nb, D_MODEL)
    st_p, st_s = [], []
    for l in range(DEPTH):
        p = _layer_params(l, lbs, w_in, b_in, conv_w, conv_b, ml_norm_g, hg_norm_g, w_out, ln1_g, ln1_b,
                          w_router_group, b_router_group, w_router_expert, b_router_expert,
                          w_gate, w_up, w_down, ln2_g, ln2_b)
        x1, conv_p, caug, m_p, s_p = _prompt_mixer(y_p, p, tt)
        n_tok = bsz * seq
        y_p = _moe(x1.reshape(n_tok, D_MODEL), p, _pick_tile(n_tok, 256)).reshape(bsz, seq, D_MODEL)
        st_p.append((conv_p, caug[..., :ML_DK], caug[..., ML_DK], m_p[:, :ML_HEADS, 0], s_p))
        x1s, conv_s, c_s, n_s, m_s, s_s = _decode_mixer(
            y_s, state_conv[l].reshape(nb, -1), state_ml_C[l], state_ml_n[l].reshape(nb, ML_WIDTH),
            state_ml_m[l], state_hg_S[l], p)
        y_s = _moe(x1s, p, _pick_tile(nb, 256))
        st_s.append((conv_s.reshape(nb, CONV_WIDTH - 1, 2 * ML_WIDTH), c_s,
                     n_s.reshape(nb, ML_HEADS, ML_DK), m_s, s_s))

    def stack(states, idx):
        return jnp.stack([s[idx] for s in states])

    return ((y_p, y_s.reshape(nb, 1, D_MODEL))
            + tuple(stack(st_p, i) for i in range(5)) + tuple(stack(st_s, i) for i in range(5)))
```

```python
import functools

import jax
import jax.numpy as jnp
from jax import lax
from jax.experimental import pallas as pl
from jax.experimental.pallas import tpu as pltpu

F32 = jnp.float32
BF16 = jnp.bfloat16

D_MODEL = 1024
DEPTH = 2
ML_HEADS = 4
ML_WIDTH = 512
ML_DK = 128
ML_CHUNK = 128
CONV_WIDTH = 4
HG_HEADS = 4
HG_WIDTH = 512
HG_DK = 128
HG_CHUNK = 64
ML_COLS = 4 * ML_WIDTH + 2 * ML_HEADS
N_GROUPS = 4
EXPERTS_PER_GROUP = 4
N_EXPERTS = 16
EXPERT_FF = 256
DEEPNORM_ALPHA = (2 * DEPTH) ** 0.25
LN_EPS = 1e-5
RMS_EPS = 1e-6

LANES = 128
SUBLANES = 8
MAIN_COLS = 4 * ML_WIDTH + 4 * HG_WIDTH
GATE_COLS = 2 * LANES
CONV_PAD = SUBLANES
HG_SAFE_DECAY = 150.0
VMEM_LIMIT = 56 * 1024 * 1024


def _dot(a, b):
    return jnp.dot(a, b, preferred_element_type=F32)


def _dot_nt(a, b):
    return lax.dot_general(a, b, (((1,), (1,)), ((), ())), preferred_element_type=F32)


def _dot_tn(a, b):
    return lax.dot_general(a, b, (((0,), (0,)), ((), ())), preferred_element_type=F32)


def _sigmoid(x):
    return 1.0 / (1.0 + jnp.exp(-x))


def _silu(x):
    return x * _sigmoid(x)


def _log_sigmoid(x):
    return jnp.minimum(x, 0.0) - jnp.log(1.0 + jnp.exp(-jnp.abs(x)))


def _cumsum_rows(tri_bf, x):
    hi = x.astype(BF16)
    lo = (x - hi.astype(F32)).astype(BF16)
    return _dot(tri_bf, hi) + _dot(tri_bf, lo)


def _layer_norm_rows(x, g, b):
    mu = jnp.mean(x, axis=-1, keepdims=True)
    xc = x - mu
    var = jnp.mean(xc * xc, axis=-1, keepdims=True)
    return xc * lax.rsqrt(var + LN_EPS) * g + b


def _head_rms(h, g):
    return h * lax.rsqrt(jnp.mean(h * h, axis=-1, keepdims=True) + RMS_EPS) * g


def _hg_gates(hf, log_lb, log1m_lb, one_m_lb):
    e = jnp.exp(-jnp.abs(hf))
    one_p_e = 1.0 + e
    log_sig = jnp.minimum(hf, 0.0) - jnp.log(one_p_e)
    b = log1m_lb + log_sig
    hi = jnp.maximum(log_lb, b)
    lo = jnp.minimum(log_lb, b)
    lf = hi + jnp.log(1.0 + jnp.exp(lo - hi))
    k = one_m_lb * jnp.where(hf >= 0.0, e, 1.0) / one_p_e
    return lf, k


def _prompt_mixer_kernel(x_ref, w_main_ref, w_gate_ref, b_main_ref, b_gate_ref, conv_w_ref, conv_b_ref,
                         ml_g_ref, hg_g_ref, log_lb_ref, log1m_lb_ref, one_m_lb_ref, w_out_ref,
                         ln_g_ref, ln_b_ref,
                         x1_ref, conv_out_ref, caug_out_ref, m_out_ref, s_out_ref,
                         xp_scr, caug_scr, m_scr, st_scr, mixed_scr, hq_scr, hk_scr, hi_scr, hgc_scr, ho_scr,
                         hinter_scr, *, tt):
    t = pl.program_id(1)
    nt = pl.num_programs(1)

    @pl.when(t == 0)
    def _():
        xp_scr[0:CONV_PAD, :] = jnp.zeros((CONV_PAD, 2 * ML_WIDTH), F32)
        caug_scr[...] = jnp.zeros_like(caug_scr)
        m_scr[...] = jnp.zeros_like(m_scr)
        st_scr[...] = jnp.zeros_like(st_scr)

    x = x_ref[0]
    xb = x.astype(BF16)

    def proj(lo, hi):
        return _dot(xb, w_main_ref[:, lo:hi]) + b_main_ref[:, lo:hi]

    xp_scr[CONV_PAD:CONV_PAD + tt, :] = proj(0, 2 * ML_WIDTH)
    qk = conv_b_ref[...] + xp_scr[CONV_PAD - 3:CONV_PAD - 3 + tt, :] * conv_w_ref[0:1, :]
    for j in range(1, CONV_WIDTH):
        qk = qk + xp_scr[CONV_PAD - 3 + j:CONV_PAD - 3 + j + tt, :] * conv_w_ref[j:j + 1, :]
    qk = _silu(qk)
    last_rows = xp_scr[CONV_PAD + tt - 3:CONV_PAD + tt, :]
    xp_scr[CONV_PAD - 3:CONV_PAD, :] = last_rows

    q_all = qk[:, :ML_WIDTH].astype(BF16)
    k_all = qk[:, ML_WIDTH:] * (ML_DK ** -0.5)
    v_all = proj(2 * ML_WIDTH, 3 * ML_WIDTH).astype(BF16)
    o_gate = _sigmoid(proj(3 * ML_WIDTH, 4 * ML_WIDTH))
    zg = _dot(xb, w_gate_ref[...]) + b_gate_ref[...]

    row = lax.broadcasted_iota(jnp.int32, (ML_CHUNK, ML_CHUNK), 0)
    col = lax.broadcasted_iota(jnp.int32, (ML_CHUNK, ML_CHUNK), 1)
    tril = row >= col
    tri_bf = jnp.where(tril, 1.0, 0.0).astype(BF16)
    ones_col = jnp.where(lax.broadcasted_iota(jnp.int32, (ML_CHUNK, LANES), 1) == 0, 1.0, 0.0).astype(BF16)

    for c in range(tt // ML_CHUNK):
        r0 = c * ML_CHUNK
        li_s = zg[r0:r0 + ML_CHUNK, 0:LANES]
        lf_s = _log_sigmoid(zg[r0:r0 + ML_CHUNK, LANES:2 * LANES])
        f_s = _cumsum_rows(tri_bf, lf_s)
        a_s = li_s - f_s
        a_t = a_s.T
        for h in range(ML_HEADS):
            hs = slice(h * ML_DK, (h + 1) * ML_DK)
            q = q_all[r0:r0 + ML_CHUNK, hs]
            k = k_all[r0:r0 + ML_CHUNK, hs]
            vaug = jnp.concatenate([v_all[r0:r0 + ML_CHUNK, hs], ones_col], axis=1)
            f_c = f_s[:, h:h + 1]
            a_c = a_s[:, h:h + 1]
            a_r = a_t[h:h + 1, :]
            m_prev = m_scr[h:h + 1, 0:1]
            cmax = jnp.max(jnp.where(tril, a_r, -jnp.inf), axis=1, keepdims=True)
            m_t = f_c + jnp.maximum(m_prev, cmax)
            dmat = jnp.where(tril, jnp.exp((f_c - m_t) + a_r), 0.0)
            s = _dot_nt(q, k.astype(BF16)) * dmat
            dec = jnp.exp(f_c + m_prev - m_t)
            caug = caug_scr[h]
            tot = _dot(s.astype(BF16), vaug) + dec * _dot(q, caug.astype(BF16))
            num = tot[:, :ML_DK]
            qn = tot[:, ML_DK:ML_DK + 1]
            hout = num / jnp.maximum(jnp.abs(qn), jnp.exp(-m_t))
            f_end = f_c[ML_CHUNK - 1:ML_CHUNK, :]
            m_end = m_t[ML_CHUNK - 1:ML_CHUNK, :]
            w_end = jnp.exp(f_end + a_c - m_end)
            dec_end = jnp.exp(f_end + m_prev - m_end)
            caug_scr[h] = dec_end * caug + _dot_tn((k * w_end).astype(BF16), vaug)
            m_scr[h:h + 1, :] = jnp.broadcast_to(m_end, (1, LANES))
            hout = _head_rms(o_gate[r0:r0 + ML_CHUNK, hs] * hout, ml_g_ref[:, hs])
            mixed_scr[r0:r0 + ML_CHUNK, hs] = hout.astype(BF16)

    base = 4 * ML_WIDTH
    hq_scr[...] = _silu(proj(base, base + HG_WIDTH))
    lf_h, k_h = _hg_gates(proj(base + HG_WIDTH, base + 2 * HG_WIDTH),
                          log_lb_ref[...], log1m_lb_ref[...], one_m_lb_ref[...])
    hk_scr[...] = k_h
    hi_scr[...] = proj(base + 2 * HG_WIDTH, base + 3 * HG_WIDTH)
    rt = lax.broadcasted_iota(jnp.int32, (tt, tt), 0)
    ct = lax.broadcasted_iota(jnp.int32, (tt, tt), 1)
    blk_tri = jnp.where((rt >= ct) & ((rt >> 6) == (ct >> 6)), 1.0, 0.0).astype(BF16)
    g_all = _cumsum_rows(blk_tri, lf_h)
    hgc_scr[...] = g_all
    tril64 = tril[:HG_CHUNK, :HG_CHUNK]
    row64 = row[:HG_CHUNK, :1]

    g_min = None
    for c in range(tt // HG_CHUNK):
        r0 = c * HG_CHUNK
        rows = slice(r0, r0 + HG_CHUNK)
        for h in range(HG_HEADS):
            hs = slice(h * HG_DK, (h + 1) * HG_DK)
            g = g_all[rows, hs]
            q = hq_scr[rows, hs]
            k = k_h[rows, hs]
            iv = hi_scr[rows, hs].astype(BF16)
            g_end = g[HG_CHUNK - 1:HG_CHUNK, :]
            g_min = g_end if g_min is None else jnp.minimum(g_min, g_end)
            st = st_scr[h]
            inter = _dot_nt((q * jnp.exp(g)).astype(BF16), st.astype(BF16))
            ref = 0.5 * g_end
            qa = (q * jnp.exp(g - ref)).astype(BF16)
            kb = (k * jnp.exp(ref - g)).astype(BF16)
            amat = jnp.where(tril64, _dot_nt(qa, kb), 0.0)
            hinter_scr[rows, hs] = inter
            ho_scr[rows, hs] = _dot(amat.astype(BF16), iv) + inter
            st_scr[h] = jnp.exp(g_end) * st + _dot_tn(iv, (k * jnp.exp(g_end - g)).astype(BF16))

    hg_gate = _silu(proj(base + 3 * HG_WIDTH, base + 4 * HG_WIDTH))

    def finish():
        for h in range(HG_HEADS):
            hs = slice(h * HG_DK, (h + 1) * HG_DK)
            o = _head_rms(ho_scr[:, hs], hg_g_ref[:, hs]) * hg_gate[:, hs]
            mixed_scr[:, ML_WIDTH + h * HG_DK:ML_WIDTH + (h + 1) * HG_DK] = o.astype(BF16)
        mix = _dot(mixed_scr[...], w_out_ref[...])
        x1_ref[0] = _layer_norm_rows(DEEPNORM_ALPHA * x + mix, ln_g_ref[...], ln_b_ref[...])

    finish()

    @pl.when(jnp.min(g_min) <= -HG_SAFE_DECAY)
    def _():
        for c in range(tt // HG_CHUNK):
            r0 = c * HG_CHUNK
            rows = slice(r0, r0 + HG_CHUNK)
            for h in range(HG_HEADS):
                hs = slice(h * HG_DK, (h + 1) * HG_DK)
                g = hgc_scr[rows, hs]
                k = hk_scr[rows, hs]
                ivf = hi_scr[rows, hs]

                def body(grp, carry):
                    rows8 = pl.ds(pl.multiple_of(r0 + grp * SUBLANES, SUBLANES), SUBLANES)
                    g8 = hgc_scr[rows8, hs]
                    q8 = hq_scr[rows8, hs]
                    outs = []
                    for jj in range(SUBLANES):
                        w = jnp.exp(jnp.minimum(g8[jj:jj + 1, :] - g, 0.0)) * k * q8[jj:jj + 1, :]
                        a_col = jnp.sum(w, axis=1, keepdims=True)
                        a_col = jnp.where(row64 <= grp * SUBLANES + jj, a_col, 0.0)
                        outs.append(jnp.sum(a_col * ivf, axis=0, keepdims=True))
                    ho_scr[rows8, hs] = jnp.concatenate(outs, axis=0) + hinter_scr[rows8, hs]
                    return carry

                lax.fori_loop(0, HG_CHUNK // SUBLANES, body, 0)
        finish()

    @pl.when(t == nt - 1)
    def _():
        for h in range(ML_HEADS):
            caug_out_ref[0, h] = caug_scr[h]
            s_out_ref[0, h] = st_scr[h].T
        m_out_ref[0] = m_scr[...]
        conv_out_ref[0] = xp_scr[CONV_PAD - 3:CONV_PAD, :]


def _const_spec(shape):
    nd = len(shape)
    return pl.BlockSpec(shape, lambda *_: (0,) * nd, pipeline_mode=pl.Buffered(1))


def _skip_alias_refs(body, n_in, n_alias, *refs):
    return body(*refs[:n_in], *refs[n_in + n_alias:])


def _alias_args(prev, n_in):
    if prev is None:
        return [], (), {}
    specs = [pl.BlockSpec(memory_space=pl.ANY)] * len(prev)
    return specs, tuple(prev), {n_in + k: 1 + k for k in range(len(prev))}


def _prompt_mixer(x, p, tt, layer, prev):
    bsz, seq, _ = x.shape
    consts = (p["w_main"], p["w_gate"], p["b_main"], p["b_gate"], p["conv_w"], p["conv_b"], p["ml_g"],
              p["hg_g"], p["log_lb"], p["log1m_lb"], p["one_m_lb"], p["w_out"], p["ln1_g"], p["ln1_b"])
    out_shape = (
        jax.ShapeDtypeStruct((bsz, seq, D_MODEL), F32),
        jax.ShapeDtypeStruct((DEPTH, bsz, CONV_WIDTH - 1, 2 * ML_WIDTH), F32),
        jax.ShapeDtypeStruct((DEPTH, bsz, ML_HEADS, ML_DK, 2 * ML_DK), F32),
        jax.ShapeDtypeStruct((DEPTH, bsz, SUBLANES, LANES), F32),
        jax.ShapeDtypeStruct((DEPTH, bsz, HG_HEADS, HG_DK, HG_DK), F32),
    )
    out_specs = (
        pl.BlockSpec((1, tt, D_MODEL), lambda b, t: (b, t, 0)),
        pl.BlockSpec((None, 1, CONV_WIDTH - 1, 2 * ML_WIDTH), lambda b, t: (layer, b, 0, 0)),
        pl.BlockSpec((None, 1, ML_HEADS, ML_DK, 2 * ML_DK), lambda b, t: (layer, b, 0, 0, 0)),
        pl.BlockSpec((None, 1, SUBLANES, LANES), lambda b, t: (layer, b, 0, 0)),
        pl.BlockSpec((None, 1, HG_HEADS, HG_DK, HG_DK), lambda b, t: (layer, b, 0, 0, 0)),
    )
    n_in = 1 + len(consts)
    alias_specs, alias_ops, alias_map = _alias_args(prev, n_in)
    scratch = [
        pltpu.VMEM((CONV_PAD + tt, 2 * ML_WIDTH), F32),
        pltpu.VMEM((ML_HEADS, ML_DK, 2 * ML_DK), F32),
        pltpu.VMEM((SUBLANES, LANES), F32),
        pltpu.VMEM((HG_HEADS, HG_DK, HG_DK), F32),
        pltpu.VMEM((tt, 2 * ML_WIDTH), BF16),
        pltpu.VMEM((tt, HG_WIDTH), F32),
        pltpu.VMEM((tt, HG_WIDTH), F32),
        pltpu.VMEM((tt, HG_WIDTH), F32),
        pltpu.VMEM((tt, HG_WIDTH), F32),
        pltpu.VMEM((tt, HG_WIDTH), F32),
        pltpu.VMEM((tt, HG_WIDTH), F32),
    ]
    return pl.pallas_call(
        functools.partial(_skip_alias_refs, functools.partial(_prompt_mixer_kernel, tt=tt), n_in, len(alias_ops)),
        grid=(bsz, seq // tt),
        in_specs=([pl.BlockSpec((1, tt, D_MODEL), lambda b, t: (b, t, 0))] + [_const_spec(c.shape) for c in consts]
                  + alias_specs),
        out_specs=out_specs,
        out_shape=out_shape,
        scratch_shapes=scratch,
        input_output_aliases=alias_map,
        compiler_params=pltpu.CompilerParams(dimension_semantics=("arbitrary", "arbitrary"),
                                             vmem_limit_bytes=VMEM_LIMIT),
        name="prompt_mixer",
    )(x, *consts, *alias_ops)


DEC_BLOCK = SUBLANES


def _decode_mixer_kernel(x_ref, conv_ref, c_ref, n_ref, m_ref, s_ref,
                         w_main_ref, w_gate_ref, b_main_ref, b_gate_ref, conv_w_ref, conv_b_ref,
                         ml_g_ref, hg_g_ref, log_lb_ref, log1m_lb_ref, one_m_lb_ref, w_out_ref,
                         ln_g_ref, ln_b_ref,
                         x1_ref, conv_out_ref, c_out_ref, n_out_ref, m_out_ref, s_out_ref,
                         q_scr, k_scr, v_scr, og_scr, li_scr, lf_scr, hq_scr, hk_scr, hf_scr, hi_scr, hgate_scr,
                         qt_scr, kt_scr, hqt_scr, hkt_scr, hft_scr, hml_scr, ho_scr, *, nb):
    i = pl.program_id(0)
    nsteps = pl.num_programs(0)

    @pl.when(i == 0)
    def _():
        xb = x_ref[...].astype(BF16)

        def proj(lo, hi):
            return _dot(xb, w_main_ref[:, lo:hi]) + b_main_ref[:, lo:hi]

        qk_raw = proj(0, 2 * ML_WIDTH)
        w = 2 * ML_WIDTH
        qk = conv_b_ref[...] + qk_raw * conv_w_ref[CONV_WIDTH - 1:CONV_WIDTH, :]
        for j in range(CONV_WIDTH - 1):
            qk = qk + conv_ref[:, j * w:(j + 1) * w] * conv_w_ref[j:j + 1, :]
        qk = _silu(qk)
        conv_out_ref[:, 0:2 * w] = conv_ref[:, w:3 * w]
        conv_out_ref[:, 2 * w:3 * w] = qk_raw
        q_scr[...] = qk[:, :ML_WIDTH]
        k_scr[...] = qk[:, ML_WIDTH:] * (ML_DK ** -0.5)
        v_scr[...] = proj(2 * ML_WIDTH, 3 * ML_WIDTH)
        og_scr[...] = _sigmoid(proj(3 * ML_WIDTH, 4 * ML_WIDTH))
        zg = _dot(xb, w_gate_ref[...]) + b_gate_ref[...]
        li_scr[...] = zg[:, 0:LANES]
        lf_scr[...] = _log_sigmoid(zg[:, LANES:2 * LANES])
        base = 4 * ML_WIDTH
        hq_scr[...] = _silu(proj(base, base + HG_WIDTH))
        lf_h, k_h = _hg_gates(proj(base + HG_WIDTH, base + 2 * HG_WIDTH),
                              log_lb_ref[...], log1m_lb_ref[...], one_m_lb_ref[...])
        hk_scr[...] = k_h
        hf_scr[...] = jnp.exp(lf_h)
        hi_scr[...] = proj(base + 2 * HG_WIDTH, base + 3 * HG_WIDTH)
        hgate_scr[...] = _silu(proj(base + 3 * HG_WIDTH, base + 4 * HG_WIDTH))
        for h in range(ML_HEADS):
            hs = slice(h * ML_DK, (h + 1) * ML_DK)
            qt_scr[h] = q_scr[:, hs].T
            kt_scr[h] = k_scr[:, hs].T
            hqt_scr[h] = hq_scr[:, hs].T
            hkt_scr[h] = hk_scr[:, hs].T
            hft_scr[h] = hf_scr[:, hs].T

    b0 = pl.multiple_of(i * DEC_BLOCK, DEC_BLOCK)
    shift = (nb - b0) & (nb - 1)
    rows = pl.ds(b0, DEC_BLOCK)
    li = li_scr[rows, :]
    lf = lf_scr[rows, :]
    m_prev = m_ref[...]
    q_blk = q_scr[rows, :]
    k_blk = k_scr[rows, :]
    v_blk = v_scr[rows, :]
    hi_blk = hi_scr[rows, :]
    n_blk = n_ref[...]
    m_new_cols = []
    for h in range(ML_HEADS):
        hs = slice(h * ML_DK, (h + 1) * ML_DK)
        li_h = li[:, h:h + 1]
        lf_h = lf[:, h:h + 1]
        mp_h = m_prev[:, h:h + 1]
        m_t = jnp.maximum(lf_h + mp_h, li_h)
        dmat = jnp.exp(li_h - m_t)
        dec = jnp.exp(lf_h + mp_h - m_t)
        n_h = n_blk[:, hs]
        s = jnp.sum(q_blk[:, hs] * k_blk[:, hs], axis=1, keepdims=True) * dmat
        qn = s + dec * jnp.sum(q_blk[:, hs] * n_h, axis=1, keepdims=True)
        denom = jnp.maximum(jnp.abs(qn), jnp.exp(-m_t))
        n_out_ref[:, hs] = dec * n_h + dmat * k_blk[:, hs]
        m_new_cols.append(m_t)
        qt = pltpu.roll(qt_scr[h], shift, 1)
        kt = pltpu.roll(kt_scr[h], shift, 1)
        hqt = pltpu.roll(hqt_scr[h], shift, 1)
        hkt = pltpu.roll(hkt_scr[h], shift, 1)
        hft = pltpu.roll(hft_scr[h], shift, 1)
        qc_rows, ho_rows = [], []
        for j in range(DEC_BLOCK):
            jr = slice(j, j + 1)
            cmat = c_ref[j, h]
            qc_rows.append(jnp.sum(qt[:, jr] * cmat, axis=0, keepdims=True))
            c_out_ref[j, h] = dec[jr, :] * cmat + (dmat[jr, :] * kt[:, jr]) * v_blk[jr, hs]
            s_new = hft[:, jr] * s_ref[j, h] + hkt[:, jr] * hi_blk[jr, hs]
            s_out_ref[j, h] = s_new
            ho_rows.append(jnp.sum(hqt[:, jr] * s_new, axis=0, keepdims=True))
        num = s * v_blk[:, hs] + dec * jnp.concatenate(qc_rows, axis=0)
        hml_scr[rows, hs] = num / denom
        ho_scr[rows, hs] = jnp.concatenate(ho_rows, axis=0)
    m_out_ref[...] = jnp.concatenate(m_new_cols, axis=1)

    @pl.when(i == nsteps - 1)
    def _():
        for h in range(ML_HEADS):
            hs = slice(h * ML_DK, (h + 1) * ML_DK)
            hm = _head_rms(og_scr[:, hs] * hml_scr[:, hs], ml_g_ref[:, hs])
            hml_scr[:, hs] = hm
            ho = _head_rms(ho_scr[:, hs], hg_g_ref[:, hs]) * hgate_scr[:, hs]
            ho_scr[:, hs] = ho
        mix = (_dot(hml_scr[...].astype(BF16), w_out_ref[0:ML_WIDTH, :])
               + _dot(ho_scr[...].astype(BF16), w_out_ref[ML_WIDTH:2 * ML_WIDTH, :]))
        x1_ref[...] = _layer_norm_rows(DEEPNORM_ALPHA * x_ref[...] + mix, ln_g_ref[...], ln_b_ref[...])


def _decode_mixer(x, conv, c_state, n_state, m_state, s_state, p, layer, prev):
    nb = x.shape[0]
    assert nb == LANES, "decode batch must fill one lane tile"
    consts = (p["w_main"], p["w_gate"], p["b_main"], p["b_gate"], p["conv_w"], p["conv_b"], p["ml_g"],
              p["hg_g"], p["log_lb"], p["log1m_lb"], p["one_m_lb"], p["w_out"], p["ln1_g"], p["ln1_b"])
    conv_cols = (CONV_WIDTH - 1) * 2 * ML_WIDTH
    blk4 = lambda i: (layer, i, 0, 0, 0)
    blk2 = lambda i: (layer, i, 0)
    conv_spec = pl.BlockSpec((None, nb, conv_cols), lambda i: (layer, 0, 0))
    state_specs = [
        pl.BlockSpec((None, DEC_BLOCK, ML_HEADS, ML_DK, ML_DK), blk4),
        pl.BlockSpec((None, DEC_BLOCK, ML_WIDTH), blk2),
        pl.BlockSpec((None, DEC_BLOCK, ML_HEADS), blk2),
        pl.BlockSpec((None, DEC_BLOCK, HG_HEADS, HG_DK, HG_DK), blk4),
    ]
    full2 = lambda shape: pl.BlockSpec(shape, lambda i: (0, 0))
    out_shape = (
        jax.ShapeDtypeStruct((nb, D_MODEL), F32),
        jax.ShapeDtypeStruct(conv.shape, F32),
        jax.ShapeDtypeStruct(c_state.shape, F32),
        jax.ShapeDtypeStruct(n_state.shape, F32),
        jax.ShapeDtypeStruct(m_state.shape, F32),
        jax.ShapeDtypeStruct(s_state.shape, F32),
    )
    n_in = 6 + len(consts)
    alias_specs, alias_ops, alias_map = _alias_args(prev, n_in)
    rows_f32 = lambda w: pltpu.VMEM((nb, w), F32)
    scratch = ([rows_f32(ML_WIDTH)] * 4 + [rows_f32(LANES)] * 2 + [rows_f32(HG_WIDTH)] * 5
               + [pltpu.VMEM((ML_HEADS, ML_DK, nb), F32)] * 5 + [rows_f32(ML_WIDTH), rows_f32(HG_WIDTH)])
    return pl.pallas_call(
        functools.partial(_skip_alias_refs, functools.partial(_decode_mixer_kernel, nb=nb), n_in, len(alias_ops)),
        grid=(nb // DEC_BLOCK,),
        in_specs=([full2((nb, D_MODEL)), conv_spec] + state_specs + [_const_spec(c.shape) for c in consts]
                  + alias_specs),
        out_specs=(full2((nb, D_MODEL)), conv_spec) + tuple(state_specs),
        out_shape=out_shape,
        scratch_shapes=scratch,
        input_output_aliases=alias_map,
        compiler_params=pltpu.CompilerParams(dimension_semantics=("arbitrary",), vmem_limit_bytes=VMEM_LIMIT),
        name="decode_mixer",
    )(x, conv, c_state, n_state, m_state, s_state, *consts, *alias_ops)


def _moe_kernel(x_ref, w_r_ref, b_r_ref, w1_ref, w3_ref, w2_ref, ln_g_ref, ln_b_ref, y_ref, hdn_scr):
    x = x_ref[...]
    xb = x.astype(BF16)
    logits = _dot(xb, w_r_ref[...]) + b_r_ref[...]
    lg = logits[:, 0:N_GROUPS]
    le = logits[:, N_GROUPS:N_GROUPS + N_EXPERTS]
    gidx = lax.broadcasted_iota(jnp.int32, lg.shape, 1)
    gmax = jnp.max(lg, axis=1, keepdims=True)
    g = jnp.min(jnp.where(lg == gmax, gidx, N_GROUPS), axis=1, keepdims=True)
    pg = 1.0 / jnp.sum(jnp.exp(lg - gmax), axis=1, keepdims=True)
    eidx = lax.broadcasted_iota(jnp.int32, le.shape, 1)
    in_g = (eidx >> 2) == g
    lm = jnp.where(in_g, le, -jnp.inf)
    v1 = jnp.max(lm, axis=1, keepdims=True)
    i1 = jnp.min(jnp.where(lm == v1, eidx, N_EXPERTS), axis=1, keepdims=True)
    lm2 = jnp.where(eidx == i1, -jnp.inf, lm)
    v2 = jnp.max(lm2, axis=1, keepdims=True)
    i2 = jnp.min(jnp.where(lm2 == v2, eidx, N_EXPERTS), axis=1, keepdims=True)
    e21 = jnp.exp(v2 - v1)
    wt1 = pg / (1.0 + e21)
    wt2 = wt1 * e21
    gate = jnp.where(eidx == i1, wt1, 0.0) + jnp.where(eidx == i2, wt2, 0.0)
    for e in range(N_EXPERTS):
        a = _dot(xb, w1_ref[e])
        u = _dot(xb, w3_ref[e])
        hdn = _silu(a) * u * gate[:, e:e + 1]
        hdn_scr[:, e * EXPERT_FF:(e + 1) * EXPERT_FF] = hdn.astype(BF16)
    moe = _dot(hdn_scr[...], w2_ref[...])
    y_ref[...] = _layer_norm_rows(DEEPNORM_ALPHA * x + moe, ln_g_ref[...], ln_b_ref[...])


def _moe(x, p, tm):
    n = x.shape[0]
    consts = (p["w_router"], p["b_router"], p["w_gate_e"], p["w_up_e"], p["w_down_e"], p["ln2_g"], p["ln2_b"])
    return pl.pallas_call(
        _moe_kernel,
        grid=(n // tm,),
        in_specs=[pl.BlockSpec((tm, D_MODEL), lambda i: (i, 0))] + [_const_spec(c.shape) for c in consts],
        out_specs=pl.BlockSpec((tm, D_MODEL), lambda i: (i, 0)),
        out_shape=jax.ShapeDtypeStruct((n, D_MODEL), F32),
        scratch_shapes=[pltpu.VMEM((tm, N_EXPERTS * EXPERT_FF), BF16)],
        compiler_params=pltpu.CompilerParams(dimension_semantics=("arbitrary",), vmem_limit_bytes=VMEM_LIMIT),
        name="moe",
    )(x, *consts)


def _layer_params(l, lbs, w_in, b_in, conv_w, conv_b, ml_norm_g, hg_norm_g, w_out, ln1_g, ln1_b,
                  w_router_group, b_router_group, w_router_expert, b_router_expert,
                  w_gate, w_up, w_down, ln2_g, ln2_b):
    g0 = 4 * ML_WIDTH
    w = w_in[l]
    b = b_in[l]
    w_main = jnp.concatenate([w[:, :g0], w[:, ML_COLS:]], axis=1).astype(BF16)
    b_main = jnp.concatenate([b[:g0], b[ML_COLS:]])[None, :]
    zw = jnp.zeros((D_MODEL, LANES - ML_HEADS), F32)
    zb = jnp.zeros((LANES - ML_HEADS,), F32)
    w_gate_cols = jnp.concatenate([w[:, g0:g0 + ML_HEADS], zw, w[:, g0 + ML_HEADS:ML_COLS], zw], axis=1).astype(BF16)
    b_gate_cols = jnp.concatenate([b[g0:g0 + ML_HEADS], zb, b[g0 + ML_HEADS:ML_COLS], zb])[None, :]
    pad_r = LANES - N_GROUPS - N_EXPERTS
    w_router = jnp.concatenate([w_router_group[l], w_router_expert[l], jnp.zeros((D_MODEL, pad_r), F32)],
                               axis=1).astype(BF16)
    b_router = jnp.concatenate([b_router_group[l], b_router_expert[l], jnp.zeros((pad_r,), F32)])[None, :]
    lb = lbs[l]
    return {
        "w_main": w_main, "b_main": b_main, "w_gate": w_gate_cols, "b_gate": b_gate_cols,
        "conv_w": conv_w[l], "conv_b": conv_b[l][None, :], "ml_g": ml_norm_g[l][None, :],
        "hg_g": hg_norm_g[l][None, :], "log_lb": jnp.log(lb)[None, :], "log1m_lb": jnp.log1p(-lb)[None, :],
        "one_m_lb": (1.0 - lb)[None, :], "w_out": w_out[l].astype(BF16),
        "ln1_g": ln1_g[l][None, :], "ln1_b": ln1_b[l][None, :],
        "w_router": w_router, "b_router": b_router,
        "w_gate_e": w_gate[l].astype(BF16), "w_up_e": w_up[l].astype(BF16),
        "w_down_e": w_down[l].astype(BF16).reshape(N_EXPERTS * EXPERT_FF, D_MODEL),
        "ln2_g": ln2_g[l][None, :], "ln2_b": ln2_b[l][None, :],
    }


def _pick_tile(n, pref):
    t = pref
    while n % t:
        t //= 2
    return t


def kernel(x_prompt, x_sample, state_conv, state_ml_C, state_ml_n, state_ml_m, state_hg_S, w_in, b_in, conv_w, conv_b, ml_norm_g, hg_lower_bounds, hg_norm_g, w_out, ln1_g, ln1_b, w_router_group, b_router_group, w_router_expert, b_router_expert, w_gate, w_up, w_down, ln2_g, ln2_b):
    sm = jax.nn.softmax(hg_lower_bounds.astype(F32), axis=0)
    lbs = jnp.concatenate([jnp.zeros_like(sm[:1]), jnp.cumsum(sm[1:], axis=0)], axis=0)
    bsz, seq, _ = x_prompt.shape
    nb = x_sample.shape[0]
    tt = _pick_tile(seq, 256)
    y_p = x_prompt
    y_s = x_sample.reshape(nb, D_MODEL)
    n_tok = bsz * seq
    conv_in = state_conv.reshape(DEPTH, nb, -1)
    n_in = state_ml_n.reshape(DEPTH, nb, ML_WIDTH)
    st_p = st_s = None
    for l in range(DEPTH):
        p = _layer_params(l, lbs, w_in, b_in, conv_w, conv_b, ml_norm_g, hg_norm_g, w_out, ln1_g, ln1_b,
                          w_router_group, b_router_group, w_router_expert, b_router_expert,
                          w_gate, w_up, w_down, ln2_g, ln2_b)
        x1, *st_p = _prompt_mixer(y_p, p, tt, l, st_p)
        y_p = _moe(x1.reshape(n_tok, D_MODEL), p, _pick_tile(n_tok, 256)).reshape(bsz, seq, D_MODEL)
        x1s, *st_s = _decode_mixer(y_s, conv_in, state_ml_C, n_in, state_ml_m, state_hg_S, p, l, st_s)
        y_s = _moe(x1s, p, _pick_tile(nb, 256))
    conv_p, caug, m_p, s_p = st_p
    conv_s, c_s, n_s, m_s, s_s = st_s
    return (y_p, y_s.reshape(nb, 1, D_MODEL),
            conv_p, caug[..., :ML_DK], caug[..., ML_DK], m_p[:, :, :ML_HEADS, 0], s_p,
            conv_s.reshape(DEPTH, nb, CONV_WIDTH - 1, 2 * ML_WIDTH), c_s,
            n_s.reshape(DEPTH, nb, ML_HEADS, ML_DK), m_s, s_s)
```

```python
import functools

import jax
import jax.numpy as jnp
from jax import lax
from jax.experimental import pallas as pl
from jax.experimental.pallas import tpu as pltpu

F32 = jnp.float32
BF16 = jnp.bfloat16

D_MODEL = 1024
DEPTH = 2
ML_HEADS = 4
ML_WIDTH = 512
ML_DK = 128
ML_CHUNK = 128
CONV_WIDTH = 4
HG_HEADS = 4
HG_WIDTH = 512
HG_DK = 128
HG_CHUNK = 64
ML_COLS = 4 * ML_WIDTH + 2 * ML_HEADS
N_GROUPS = 4
EXPERTS_PER_GROUP = 4
N_EXPERTS = 16
EXPERT_FF = 256
DEEPNORM_ALPHA = (2 * DEPTH) ** 0.25
LN_EPS = 1e-5
RMS_EPS = 1e-6

LANES = 128
SUBLANES = 8
MAIN_COLS = 4 * ML_WIDTH + 4 * HG_WIDTH
GATE_COLS = 2 * LANES
CONV_PAD = SUBLANES
HG_SAFE_DECAY = 150.0
VMEM_LIMIT = 56 * 1024 * 1024


def _dot(a, b):
    return jnp.dot(a, b, preferred_element_type=F32)


def _dot_nt(a, b):
    return lax.dot_general(a, b, (((1,), (1,)), ((), ())), preferred_element_type=F32)


def _dot_tn(a, b):
    return lax.dot_general(a, b, (((0,), (0,)), ((), ())), preferred_element_type=F32)


def _sigmoid(x):
    return 1.0 / (1.0 + jnp.exp(-x))


def _silu(x):
    return x * _sigmoid(x)


def _log_sigmoid(x):
    return jnp.minimum(x, 0.0) - jnp.log(1.0 + jnp.exp(-jnp.abs(x)))


def _cumsum_rows(tri_bf, x):
    hi = x.astype(BF16)
    lo = (x - hi.astype(F32)).astype(BF16)
    return _dot(tri_bf, hi) + _dot(tri_bf, lo)


def _layer_norm_rows(x, g, b):
    mu = jnp.mean(x, axis=-1, keepdims=True)
    xc = x - mu
    var = jnp.mean(xc * xc, axis=-1, keepdims=True)
    return xc * lax.rsqrt(var + LN_EPS) * g + b


def _head_rms(h, g):
    return h * lax.rsqrt(jnp.mean(h * h, axis=-1, keepdims=True) + RMS_EPS) * g


def _hg_gates(hf, log_lb, log1m_lb, one_m_lb):
    e = jnp.exp(-jnp.abs(hf))
    one_p_e = 1.0 + e
    log_sig = jnp.minimum(hf, 0.0) - jnp.log(one_p_e)
    b = log1m_lb + log_sig
    hi = jnp.maximum(log_lb, b)
    lo = jnp.minimum(log_lb, b)
    lf = hi + jnp.log(1.0 + jnp.exp(lo - hi))
    k = one_m_lb * jnp.where(hf >= 0.0, e, 1.0) / one_p_e
    return lf, k


def _prompt_mixer_kernel(x_ref, w_main_ref, w_gate_ref, b_main_ref, b_gate_ref, conv_w_ref, conv_b_ref,
                         ml_g_ref, hg_g_ref, log_lb_ref, log1m_lb_ref, one_m_lb_ref, w_out_ref,
                         ln_g_ref, ln_b_ref,
                         x1_ref, conv_out_ref, caug_out_ref, m_out_ref, s_out_ref,
                         xp_scr, caug_scr, m_scr, st_scr, mixed_scr, hq_scr, hk_scr, hi_scr, hgc_scr, ho_scr,
                         hinter_scr, *, tt):
    t = pl.program_id(1)
    nt = pl.num_programs(1)

    @pl.when(t == 0)
    def _():
        xp_scr[0:CONV_PAD, :] = jnp.zeros((CONV_PAD, 2 * ML_WIDTH), F32)
        caug_scr[...] = jnp.zeros_like(caug_scr)
        m_scr[...] = jnp.zeros_like(m_scr)
        st_scr[...] = jnp.zeros_like(st_scr)

    x = x_ref[0]
    xb = x.astype(BF16)

    def proj(lo, hi):
        return _dot(xb, w_main_ref[:, lo:hi]) + b_main_ref[:, lo:hi]

    xp_scr[CONV_PAD:CONV_PAD + tt, :] = proj(0, 2 * ML_WIDTH)
    qk = conv_b_ref[...] + xp_scr[CONV_PAD - 3:CONV_PAD - 3 + tt, :] * conv_w_ref[0:1, :]
    for j in range(1, CONV_WIDTH):
        qk = qk + xp_scr[CONV_PAD - 3 + j:CONV_PAD - 3 + j + tt, :] * conv_w_ref[j:j + 1, :]
    qk = _silu(qk)
    last_rows = xp_scr[CONV_PAD + tt - 3:CONV_PAD + tt, :]
    xp_scr[CONV_PAD - 3:CONV_PAD, :] = last_rows

    q_all = qk[:, :ML_WIDTH].astype(BF16)
    k_all = qk[:, ML_WIDTH:] * (ML_DK ** -0.5)
    v_all = proj(2 * ML_WIDTH, 3 * ML_WIDTH).astype(BF16)
    o_gate = _sigmoid(proj(3 * ML_WIDTH, 4 * ML_WIDTH))
    zg = _dot(xb, w_gate_ref[...]) + b_gate_ref[...]

    row = lax.broadcasted_iota(jnp.int32, (ML_CHUNK, ML_CHUNK), 0)
    col = lax.broadcasted_iota(jnp.int32, (ML_CHUNK, ML_CHUNK), 1)
    tril = row >= col
    tri_bf = jnp.where(tril, 1.0, 0.0).astype(BF16)
    ones_col = jnp.where(lax.broadcasted_iota(jnp.int32, (ML_CHUNK, LANES), 1) == 0, 1.0, 0.0).astype(BF16)

    for c in range(tt // ML_CHUNK):
        r0 = c * ML_CHUNK
        li_s = zg[r0:r0 + ML_CHUNK, 0:LANES]
        lf_s = _log_sigmoid(zg[r0:r0 + ML_CHUNK, LANES:2 * LANES])
        f_s = _cumsum_rows(tri_bf, lf_s)
        a_s = li_s - f_s
        a_t = a_s.T
        for h in range(ML_HEADS):
            hs = slice(h * ML_DK, (h + 1) * ML_DK)
            q = q_all[r0:r0 + ML_CHUNK, hs]
            k = k_all[r0:r0 + ML_CHUNK, hs]
            vaug = jnp.concatenate([v_all[r0:r0 + ML_CHUNK, hs], ones_col], axis=1)
            f_c = f_s[:, h:h + 1]
            a_c = a_s[:, h:h + 1]
            a_r = a_t[h:h + 1, :]
            m_prev = m_scr[h:h + 1, 0:1]
            cmax = jnp.max(jnp.where(tril, a_r, -jnp.inf), axis=1, keepdims=True)
            m_t = f_c + jnp.maximum(m_prev, cmax)
            dmat = jnp.where(tril, jnp.exp((f_c - m_t) + a_r), 0.0)
            s = _dot_nt(q, k.astype(BF16)) * dmat
            dec = jnp.exp(f_c + m_prev - m_t)
            caug = caug_scr[h]
            tot = _dot(s.astype(BF16), vaug) + dec * _dot(q, caug.astype(BF16))
            num = tot[:, :ML_DK]
            qn = tot[:, ML_DK:ML_DK + 1]
            hout = num / jnp.maximum(jnp.abs(qn), jnp.exp(-m_t))
            f_end = f_c[ML_CHUNK - 1:ML_CHUNK, :]
            m_end = m_t[ML_CHUNK - 1:ML_CHUNK, :]
            w_end = jnp.exp(f_end + a_c - m_end)
            dec_end = jnp.exp(f_end + m_prev - m_end)
            caug_scr[h] = dec_end * caug + _dot_tn((k * w_end).astype(BF16), vaug)
            m_scr[h:h + 1, :] = jnp.broadcast_to(m_end, (1, LANES))
            hout = _head_rms(o_gate[r0:r0 + ML_CHUNK, hs] * hout, ml_g_ref[:, hs])
            mixed_scr[r0:r0 + ML_CHUNK, hs] = hout.astype(BF16)

    base = 4 * ML_WIDTH
    hq_scr[...] = _silu(proj(base, base + HG_WIDTH))
    lf_h, k_h = _hg_gates(proj(base + HG_WIDTH, base + 2 * HG_WIDTH),
                          log_lb_ref[...], log1m_lb_ref[...], one_m_lb_ref[...])
    hk_scr[...] = k_h
    hi_scr[...] = proj(base + 2 * HG_WIDTH, base + 3 * HG_WIDTH)
    rt = lax.broadcasted_iota(jnp.int32, (tt, tt), 0)
    ct = lax.broadcasted_iota(jnp.int32, (tt, tt), 1)
    blk_tri = jnp.where((rt >= ct) & ((rt >> 6) == (ct >> 6)), 1.0, 0.0).astype(BF16)
    g_all = _cumsum_rows(blk_tri, lf_h)
    hgc_scr[...] = g_all
    tril64 = tril[:HG_CHUNK, :HG_CHUNK]
    row64 = row[:HG_CHUNK, :1]

    g_min = None
    for c in range(tt // HG_CHUNK):
        r0 = c * HG_CHUNK
        rows = slice(r0, r0 + HG_CHUNK)
        for h in range(HG_HEADS):
            hs = slice(h * HG_DK, (h + 1) * HG_DK)
            g = g_all[rows, hs]
            q = hq_scr[rows, hs]
            k = k_h[rows, hs]
            iv = hi_scr[rows, hs].astype(BF16)
            g_end = g[HG_CHUNK - 1:HG_CHUNK, :]
            g_min = g_end if g_min is None else jnp.minimum(g_min, g_end)
            st = st_scr[h]
            inter = _dot_nt((q * jnp.exp(g)).astype(BF16), st.astype(BF16))
            ref = 0.5 * g_end
            qa = (q * jnp.exp(g - ref)).astype(BF16)
            kb = (k * jnp.exp(ref - g)).astype(BF16)
            amat = jnp.where(tril64, _dot_nt(qa, kb), 0.0)
            hinter_scr[rows, hs] = inter
            ho_scr[rows, hs] = _dot(amat.astype(BF16), iv) + inter
            st_scr[h] = jnp.exp(g_end) * st + _dot_tn(iv, (k * jnp.exp(g_end - g)).astype(BF16))

    hg_gate = _silu(proj(base + 3 * HG_WIDTH, base + 4 * HG_WIDTH))

    def finish():
        for h in range(HG_HEADS):
            hs = slice(h * HG_DK, (h + 1) * HG_DK)
            o = _head_rms(ho_scr[:, hs], hg_g_ref[:, hs]) * hg_gate[:, hs]
            mixed_scr[:, ML_WIDTH + h * HG_DK:ML_WIDTH + (h + 1) * HG_DK] = o.astype(BF16)
        mix = _dot(mixed_scr[...], w_out_ref[...])
        x1_ref[0] = _layer_norm_rows(DEEPNORM_ALPHA * x + mix, ln_g_ref[...], ln_b_ref[...])

    finish()

    @pl.when(jnp.min(g_min) <= -HG_SAFE_DECAY)
    def _():
        for c in range(tt // HG_CHUNK):
            r0 = c * HG_CHUNK
            rows = slice(r0, r0 + HG_CHUNK)
            for h in range(HG_HEADS):
                hs = slice(h * HG_DK, (h + 1) * HG_DK)
                g = hgc_scr[rows, hs]
                k = hk_scr[rows, hs]
                ivf = hi_scr[rows, hs]

                def body(grp, carry):
                    rows8 = pl.ds(pl.multiple_of(r0 + grp * SUBLANES, SUBLANES), SUBLANES)
                    g8 = hgc_scr[rows8, hs]
                    q8 = hq_scr[rows8, hs]
                    outs = []
                    for jj in range(SUBLANES):
                        w = jnp.exp(jnp.minimum(g8[jj:jj + 1, :] - g, 0.0)) * k * q8[jj:jj + 1, :]
                        a_col = jnp.sum(w, axis=1, keepdims=True)
                        a_col = jnp.where(row64 <= grp * SUBLANES + jj, a_col, 0.0)
                        outs.append(jnp.sum(a_col * ivf, axis=0, keepdims=True))
                    ho_scr[rows8, hs] = jnp.concatenate(outs, axis=0) + hinter_scr[rows8, hs]
                    return carry

                lax.fori_loop(0, HG_CHUNK // SUBLANES, body, 0)
        finish()

    @pl.when(t == nt - 1)
    def _():
        for h in range(ML_HEADS):
            caug_out_ref[0, h] = caug_scr[h]
            s_out_ref[0, h] = st_scr[h].T
        m_out_ref[0] = m_scr[...]
        conv_out_ref[0] = xp_scr[CONV_PAD - 3:CONV_PAD, :]


def _const_spec(shape):
    nd = len(shape)
    return pl.BlockSpec(shape, lambda *_: (0,) * nd, pipeline_mode=pl.Buffered(1))


def _skip_alias_refs(body, n_in, n_alias, *refs):
    return body(*refs[:n_in], *refs[n_in + n_alias:])


def _alias_args(prev, n_in):
    if prev is None:
        return [], (), {}
    specs = [pl.BlockSpec(memory_space=pl.ANY)] * len(prev)
    return specs, tuple(prev), {n_in + k: 1 + k for k in range(len(prev))}


def _prompt_mixer(x, p, tt, layer, prev):
    bsz, seq, _ = x.shape
    consts = (p["w_main"], p["w_gate"], p["b_main"], p["b_gate"], p["conv_w"], p["conv_b"], p["ml_g"],
              p["hg_g"], p["log_lb"], p["log1m_lb"], p["one_m_lb"], p["w_out"], p["ln1_g"], p["ln1_b"])
    out_shape = (
        jax.ShapeDtypeStruct((bsz, seq, D_MODEL), F32),
        jax.ShapeDtypeStruct((DEPTH, bsz, CONV_WIDTH - 1, 2 * ML_WIDTH), F32),
        jax.ShapeDtypeStruct((DEPTH, bsz, ML_HEADS, ML_DK, 2 * ML_DK), F32),
        jax.ShapeDtypeStruct((DEPTH, bsz, SUBLANES, LANES), F32),
        jax.ShapeDtypeStruct((DEPTH, bsz, HG_HEADS, HG_DK, HG_DK), F32),
    )
    out_specs = (
        pl.BlockSpec((1, tt, D_MODEL), lambda b, t: (b, t, 0)),
        pl.BlockSpec((None, 1, CONV_WIDTH - 1, 2 * ML_WIDTH), lambda b, t: (layer, b, 0, 0)),
        pl.BlockSpec((None, 1, ML_HEADS, ML_DK, 2 * ML_DK), lambda b, t: (layer, b, 0, 0, 0)),
        pl.BlockSpec((None, 1, SUBLANES, LANES), lambda b, t: (layer, b, 0, 0)),
        pl.BlockSpec((None, 1, HG_HEADS, HG_DK, HG_DK), lambda b, t: (layer, b, 0, 0, 0)),
    )
    n_in = 1 + len(consts)
    alias_specs, alias_ops, alias_map = _alias_args(prev, n_in)
    scratch = [
        pltpu.VMEM((CONV_PAD + tt, 2 * ML_WIDTH), F32),
        pltpu.VMEM((ML_HEADS, ML_DK, 2 * ML_DK), F32),
        pltpu.VMEM((SUBLANES, LANES), F32),
        pltpu.VMEM((HG_HEADS, HG_DK, HG_DK), F32),
        pltpu.VMEM((tt, 2 * ML_WIDTH), BF16),
        pltpu.VMEM((tt, HG_WIDTH), F32),
        pltpu.VMEM((tt, HG_WIDTH), F32),
        pltpu.VMEM((tt, HG_WIDTH), F32),
        pltpu.VMEM((tt, HG_WIDTH), F32),
        pltpu.VMEM((tt, HG_WIDTH), F32),
        pltpu.VMEM((tt, HG_WIDTH), F32),
    ]
    return pl.pallas_call(
        functools.partial(_skip_alias_refs, functools.partial(_prompt_mixer_kernel, tt=tt), n_in, len(alias_ops)),
        grid=(bsz, seq // tt),
        in_specs=([pl.BlockSpec((1, tt, D_MODEL), lambda b, t: (b, t, 0))] + [_const_spec(c.shape) for c in consts]
                  + alias_specs),
        out_specs=out_specs,
        out_shape=out_shape,
        scratch_shapes=scratch,
        input_output_aliases=alias_map,
        compiler_params=pltpu.CompilerParams(dimension_semantics=("arbitrary", "arbitrary"),
                                             vmem_limit_bytes=VMEM_LIMIT),
        name="prompt_mixer",
    )(x, *consts, *alias_ops)


DEC_BLOCK = SUBLANES


def _decode_mixer_kernel(x_ref, conv_ref, c_ref, n_ref, m_ref, s_ref,
                         w_main_ref, w_gate_ref, b_main_ref, b_gate_ref, conv_w_ref, conv_b_ref,
                         ml_g_ref, hg_g_ref, log_lb_ref, log1m_lb_ref, one_m_lb_ref, w_out_ref,
                         ln_g_ref, ln_b_ref,
                         x1_ref, conv_out_ref, c_out_ref, n_out_ref, m_out_ref, s_out_ref,
                         q_scr, k_scr, v_scr, og_scr, li_scr, lf_scr, hq_scr, hk_scr, hf_scr, hi_scr, hgate_scr,
                         qt_scr, kt_scr, hqt_scr, hkt_scr, hft_scr, hml_scr, ho_scr, *, nb):
    i = pl.program_id(0)
    nsteps = pl.num_programs(0)

    @pl.when(i == 0)
    def _():
        xb = x_ref[...].astype(BF16)

        def proj(lo, hi):
            return _dot(xb, w_main_ref[:, lo:hi]) + b_main_ref[:, lo:hi]

        qk_raw = proj(0, 2 * ML_WIDTH)
        w = 2 * ML_WIDTH
        qk = conv_b_ref[...] + qk_raw * conv_w_ref[CONV_WIDTH - 1:CONV_WIDTH, :]
        for j in range(CONV_WIDTH - 1):
            qk = qk + conv_ref[:, j * w:(j + 1) * w] * conv_w_ref[j:j + 1, :]
        qk = _silu(qk)
        conv_out_ref[:, 0:2 * w] = conv_ref[:, w:3 * w]
        conv_out_ref[:, 2 * w:3 * w] = qk_raw
        q_scr[...] = qk[:, :ML_WIDTH]
        k_scr[...] = qk[:, ML_WIDTH:] * (ML_DK ** -0.5)
        v_scr[...] = proj(2 * ML_WIDTH, 3 * ML_WIDTH)
        og_scr[...] = _sigmoid(proj(3 * ML_WIDTH, 4 * ML_WIDTH))
        zg = _dot(xb, w_gate_ref[...]) + b_gate_ref[...]
        li_scr[...] = zg[:, 0:LANES]
        lf_scr[...] = _log_sigmoid(zg[:, LANES:2 * LANES])
        base = 4 * ML_WIDTH
        hq_scr[...] = _silu(proj(base, base + HG_WIDTH))
        lf_h, k_h = _hg_gates(proj(base + HG_WIDTH, base + 2 * HG_WIDTH),
                              log_lb_ref[...], log1m_lb_ref[...], one_m_lb_ref[...])
        hk_scr[...] = k_h
        hf_scr[...] = jnp.exp(lf_h)
        hi_scr[...] = proj(base + 2 * HG_WIDTH, base + 3 * HG_WIDTH)
        hgate_scr[...] = _silu(proj(base + 3 * HG_WIDTH, base + 4 * HG_WIDTH))
        for h in range(ML_HEADS):
            hs = slice(h * ML_DK, (h + 1) * ML_DK)
            qt_scr[h] = q_scr[:, hs].T
            kt_scr[h] = k_scr[:, hs].T
            hqt_scr[h] = hq_scr[:, hs].T
            hkt_scr[h] = hk_scr[:, hs].T
            hft_scr[h] = hf_scr[:, hs].T

    b0 = pl.multiple_of(i * DEC_BLOCK, DEC_BLOCK)
    shift = (nb - b0) & (nb - 1)
    rows = pl.ds(b0, DEC_BLOCK)
    li = li_scr[rows, :]
    lf = lf_scr[rows, :]
    m_prev = m_ref[...]
    q_blk = q_scr[rows, :]
    k_blk = k_scr[rows, :]
    v_blk = v_scr[rows, :]
    hi_blk = hi_scr[rows, :]
    n_blk = n_ref[...]
    m_new_cols = []
    for h in range(ML_HEADS):
        hs = slice(h * ML_DK, (h + 1) * ML_DK)
        li_h = li[:, h:h + 1]
        lf_h = lf[:, h:h + 1]
        mp_h = m_prev[:, h:h + 1]
        m_t = jnp.maximum(lf_h + mp_h, li_h)
        dmat = jnp.exp(li_h - m_t)
        dec = jnp.exp(lf_h + mp_h - m_t)
        n_h = n_blk[:, hs]
        s = jnp.sum(q_blk[:, hs] * k_blk[:, hs], axis=1, keepdims=True) * dmat
        qn = s + dec * jnp.sum(q_blk[:, hs] * n_h, axis=1, keepdims=True)
        denom = jnp.maximum(jnp.abs(qn), jnp.exp(-m_t))
        n_out_ref[:, hs] = dec * n_h + dmat * k_blk[:, hs]
        m_new_cols.append(m_t)
        qt = pltpu.roll(qt_scr[h], shift, 1)
        kt = pltpu.roll(kt_scr[h], shift, 1)
        hqt = pltpu.roll(hqt_scr[h], shift, 1)
        hkt = pltpu.roll(hkt_scr[h], shift, 1)
        hft = pltpu.roll(hft_scr[h], shift, 1)
        qc_rows, ho_rows = [], []
        for j in range(DEC_BLOCK):
            jr = slice(j, j + 1)
            cmat = c_ref[j, h]
            qc_rows.append(jnp.sum(qt[:, jr] * cmat, axis=0, keepdims=True))
            c_out_ref[j, h] = dec[jr, :] * cmat + (dmat[jr, :] * kt[:, jr]) * v_blk[jr, hs]
            s_new = hft[:, jr] * s_ref[j, h] + hkt[:, jr] * hi_blk[jr, hs]
            s_out_ref[j, h] = s_new
            ho_rows.append(jnp.sum(hqt[:, jr] * s_new, axis=0, keepdims=True))
        num = s * v_blk[:, hs] + dec * jnp.concatenate(qc_rows, axis=0)
        hml_scr[rows, hs] = num / denom
        ho_scr[rows, hs] = jnp.concatenate(ho_rows, axis=0)
    m_out_ref[...] = jnp.concatenate(m_new_cols, axis=1)

    @pl.when(i == nsteps - 1)
    def _():
        for h in range(ML_HEADS):
            hs = slice(h * ML_DK, (h + 1) * ML_DK)
            hm = _head_rms(og_scr[:, hs] * hml_scr[:, hs], ml_g_ref[:, hs])
            hml_scr[:, hs] = hm
            ho = _head_rms(ho_scr[:, hs], hg_g_ref[:, hs]) * hgate_scr[:, hs]
            ho_scr[:, hs] = ho
        mix = (_dot(hml_scr[...].astype(BF16), w_out_ref[0:ML_WIDTH, :])
               + _dot(ho_scr[...].astype(BF16), w_out_ref[ML_WIDTH:2 * ML_WIDTH, :]))
        x1_ref[...] = _layer_norm_rows(DEEPNORM_ALPHA * x_ref[...] + mix, ln_g_ref[...], ln_b_ref[...])


def _decode_mixer(x, conv, c_state, n_state, m_state, s_state, p, layer, prev):
    nb = x.shape[0]
    assert nb == LANES, "decode batch must fill one lane tile"
    consts = (p["w_main"], p["w_gate"], p["b_main"], p["b_gate"], p["conv_w"], p["conv_b"], p["ml_g"],
              p["hg_g"], p["log_lb"], p["log1m_lb"], p["one_m_lb"], p["w_out"], p["ln1_g"], p["ln1_b"])
    conv_cols = (CONV_WIDTH - 1) * 2 * ML_WIDTH
    blk4 = lambda i: (layer, i, 0, 0, 0)
    blk2 = lambda i: (layer, i, 0)
    conv_spec = pl.BlockSpec((None, nb, conv_cols), lambda i: (layer, 0, 0))
    state_specs = [
        pl.BlockSpec((None, DEC_BLOCK, ML_HEADS, ML_DK, ML_DK), blk4),
        pl.BlockSpec((None, DEC_BLOCK, ML_WIDTH), blk2),
        pl.BlockSpec((None, DEC_BLOCK, ML_HEADS), blk2),
        pl.BlockSpec((None, DEC_BLOCK, HG_HEADS, HG_DK, HG_DK), blk4),
    ]
    full2 = lambda shape: pl.BlockSpec(shape, lambda i: (0, 0))
    out_shape = (
        jax.ShapeDtypeStruct((nb, D_MODEL), F32),
        jax.ShapeDtypeStruct(conv.shape, F32),
        jax.ShapeDtypeStruct(c_state.shape, F32),
        jax.ShapeDtypeStruct(n_state.shape, F32),
        jax.ShapeDtypeStruct(m_state.shape, F32),
        jax.ShapeDtypeStruct(s_state.shape, F32),
    )
    n_in = 6 + len(consts)
    alias_specs, alias_ops, alias_map = _alias_args(prev, n_in)
    rows_f32 = lambda w: pltpu.VMEM((nb, w), F32)
    scratch = ([rows_f32(ML_WIDTH)] * 4 + [rows_f32(LANES)] * 2 + [rows_f32(HG_WIDTH)] * 5
               + [pltpu.VMEM((ML_HEADS, ML_DK, nb), F32)] * 5 + [rows_f32(ML_WIDTH), rows_f32(HG_WIDTH)])
    return pl.pallas_call(
        functools.partial(_skip_alias_refs, functools.partial(_decode_mixer_kernel, nb=nb), n_in, len(alias_ops)),
        grid=(nb // DEC_BLOCK,),
        in_specs=([full2((nb, D_MODEL)), conv_spec] + state_specs + [_const_spec(c.shape) for c in consts]
                  + alias_specs),
        out_specs=(full2((nb, D_MODEL)), conv_spec) + tuple(state_specs),
        out_shape=out_shape,
        scratch_shapes=scratch,
        input_output_aliases=alias_map,
        compiler_params=pltpu.CompilerParams(dimension_semantics=("arbitrary",), vmem_limit_bytes=VMEM_LIMIT),
        name="decode_mixer",
    )(x, conv, c_state, n_state, m_state, s_state, *consts, *alias_ops)


def _moe_kernel(x_ref, w_r_ref, b_r_ref, w1_ref, w3_ref, w2_ref, ln_g_ref, ln_b_ref, y_ref, hdn_scr):
    x = x_ref[...]
    xb = x.astype(BF16)
    logits = _dot(xb, w_r_ref[...]) + b_r_ref[...]
    lg = logits[:, 0:N_GROUPS]
    le = logits[:, N_GROUPS:N_GROUPS + N_EXPERTS]
    gidx = lax.broadcasted_iota(jnp.int32, lg.shape, 1)
    gmax = jnp.max(lg, axis=1, keepdims=True)
    g = jnp.min(jnp.where(lg == gmax, gidx, N_GROUPS), axis=1, keepdims=True)
    pg = 1.0 / jnp.sum(jnp.exp(lg - gmax), axis=1, keepdims=True)
    eidx = lax.broadcasted_iota(jnp.int32, le.shape, 1)
    in_g = (eidx >> 2) == g
    lm = jnp.where(in_g, le, -jnp.inf)
    v1 = jnp.max(lm, axis=1, keepdims=True)
    i1 = jnp.min(jnp.where(lm == v1, eidx, N_EXPERTS), axis=1, keepdims=True)
    lm2 = jnp.where(eidx == i1, -jnp.inf, lm)
    v2 = jnp.max(lm2, axis=1, keepdims=True)
    i2 = jnp.min(jnp.where(lm2 == v2, eidx, N_EXPERTS), axis=1, keepdims=True)
    e21 = jnp.exp(v2 - v1)
    wt1 = pg / (1.0 + e21)
    wt2 = wt1 * e21
    gate = jnp.where(eidx == i1, wt1, 0.0) + jnp.where(eidx == i2, wt2, 0.0)
    for e in range(N_EXPERTS):
        a = _dot(xb, w1_ref[e])
        u = _dot(xb, w3_ref[e])
        hdn = _silu(a) * u * gate[:, e:e + 1]
        hdn_scr[:, e * EXPERT_FF:(e + 1) * EXPERT_FF] = hdn.astype(BF16)
    moe = _dot(hdn_scr[...], w2_ref[...])
    y_ref[...] = _layer_norm_rows(DEEPNORM_ALPHA * x + moe, ln_g_ref[...], ln_b_ref[...])


def _moe(x, p, tm):
    n = x.shape[0]
    consts = (p["w_router"], p["b_router"], p["w_gate_e"], p["w_up_e"], p["w_down_e"], p["ln2_g"], p["ln2_b"])
    return pl.pallas_call(
        _moe_kernel,
        grid=(n // tm,),
        in_specs=[pl.BlockSpec((tm, D_MODEL), lambda i: (i, 0))] + [_const_spec(c.shape) for c in consts],
        out_specs=pl.BlockSpec((tm, D_MODEL), lambda i: (i, 0)),
        out_shape=jax.ShapeDtypeStruct((n, D_MODEL), F32),
        scratch_shapes=[pltpu.VMEM((tm, N_EXPERTS * EXPERT_FF), BF16)],
        compiler_params=pltpu.CompilerParams(dimension_semantics=("arbitrary",), vmem_limit_bytes=VMEM_LIMIT),
        name="moe",
    )(x, *consts)


MOE_BLOCK = 128
GROUP_FF = EXPERTS_PER_GROUP * EXPERT_FF


def _moe_grouped_kernel(x_ref, w_rt_ref, b_rt_ref, w1_ref, w3_ref, w2_ref, ln_g_ref, ln_b_ref, y_ref,
                        xb_scr, gt_scr, pos_scr, acc_scr, *, tm):
    x = x_ref[...]
    xb = x.astype(BF16)
    xb_scr[...] = xb
    logits = _dot_nt(w_rt_ref[...], xb) + b_rt_ref[...]
    ridx8 = lax.broadcasted_iota(jnp.int32, (SUBLANES, tm), 0)
    lg = jnp.where(ridx8 < N_GROUPS, logits[N_EXPERTS:N_EXPERTS + SUBLANES, :], -jnp.inf)
    gmax = jnp.max(lg, axis=0, keepdims=True)
    g = jnp.min(jnp.where(lg == gmax, ridx8, SUBLANES), axis=0, keepdims=True)
    pg = 1.0 / jnp.sum(jnp.exp(lg - gmax), axis=0, keepdims=True)
    le = logits[0:N_EXPERTS, :]
    eidx = lax.broadcasted_iota(jnp.int32, (N_EXPERTS, tm), 0)
    lm = jnp.where((eidx >> 2) == g, le, -jnp.inf)
    v1 = jnp.max(lm, axis=0, keepdims=True)
    i1 = jnp.min(jnp.where(lm == v1, eidx, N_EXPERTS), axis=0, keepdims=True)
    lm2 = jnp.where(eidx == i1, -jnp.inf, lm)
    v2 = jnp.max(lm2, axis=0, keepdims=True)
    i2 = jnp.min(jnp.where(lm2 == v2, eidx, N_EXPERTS), axis=0, keepdims=True)
    e21 = jnp.exp(v2 - v1)
    wt1 = pg / (1.0 + e21)
    wt2 = wt1 * e21
    gate = jnp.where(eidx == i1, wt1, 0.0) + jnp.where(eidx == i2, wt2, 0.0)
    gate4 = gate[0:4, :] + gate[4:8, :] + gate[8:12, :] + gate[12:16, :]
    g_hi = gate4.astype(BF16)
    g_lo = (gate4 - g_hi.astype(F32)).astype(BF16)
    gt_scr[...] = jnp.zeros_like(gt_scr)
    gt_scr[0:4, :] = g_hi
    gt_scr[SUBLANES:SUBLANES + 4, :] = g_lo

    onehot = jnp.where(ridx8 == g, 1.0, 0.0)
    tr = lax.broadcasted_iota(jnp.int32, (tm, tm), 0)
    tc = lax.broadcasted_iota(jnp.int32, (tm, tm), 1)
    before = jnp.where(tr < tc, 1.0, 0.0).astype(BF16)
    rank = _dot(onehot.astype(BF16), before)
    counts = jnp.sum(onehot, axis=1, keepdims=True).astype(jnp.int32)
    nblk_g = (counts + (MOE_BLOCK - 1)) >> 7
    s1 = nblk_g[0:1, :]
    s2 = s1 + nblk_g[1:2, :]
    s3 = s2 + nblk_g[2:3, :]
    n_blocks = s3 + nblk_g[3:4, :]
    cidx = lax.broadcasted_iota(jnp.int32, (SUBLANES, 1), 0)
    start_blk = jnp.where(cidx == 1, s1, 0) + jnp.where(cidx == 2, s2, 0) + jnp.where(cidx == 3, s3, 0)
    pos = jnp.sum(onehot * ((start_blk * MOE_BLOCK).astype(F32) + rank), axis=0, keepdims=True)
    pos_scr[...] = jnp.broadcast_to(pos, (SUBLANES, tm))
    acc_scr[...] = jnp.zeros_like(acc_scr)
    b1 = s1[0, 0]
    b2 = s2[0, 0]
    b3 = s3[0, 0]

    def block(b, carry):
        grp = jnp.where(b >= b1, 1, 0) + jnp.where(b >= b2, 1, 0) + jnp.where(b >= b3, 1, 0)
        slot = lax.broadcasted_iota(jnp.int32, (MOE_BLOCK, 1), 0) + b * MOE_BLOCK
        sel = jnp.where(pos_scr[0:1, :] == slot.astype(F32), 1.0, 0.0).astype(BF16)
        xs = _dot(sel, xb_scr[...]).astype(BF16)
        gs = _dot_nt(sel, gt_scr[...])
        gsel = gs[:, 0:4] + gs[:, SUBLANES:SUBLANES + 4]
        parts = []
        for j in range(EXPERTS_PER_GROUP):
            e = grp * EXPERTS_PER_GROUP + j
            a = _dot(xs, w1_ref[e])
            u = _dot(xs, w3_ref[e])
            parts.append((_silu(a) * u * gsel[:, j:j + 1]).astype(BF16))
        hdn = jnp.concatenate(parts, axis=1)
        ys = _dot(hdn, w2_ref[pl.ds(pl.multiple_of(grp * GROUP_FF, GROUP_FF), GROUP_FF), :])
        y_hi = ys.astype(BF16)
        y_lo = (ys - y_hi.astype(F32)).astype(BF16)
        acc_scr[...] += _dot_tn(jnp.concatenate([sel, sel], axis=0), jnp.concatenate([y_hi, y_lo], axis=0))
        return carry

    lax.fori_loop(0, n_blocks[0, 0], block, 0)
    y_ref[...] = _layer_norm_rows(DEEPNORM_ALPHA * x + acc_scr[...], ln_g_ref[...], ln_b_ref[...])


def _moe_grouped(x, p, tm):
    n = x.shape[0]
    consts = (p["w_router_t"], p["b_router_t"], p["w_gate_e"], p["w_up_e"], p["w_down_e"], p["ln2_g"], p["ln2_b"])
    return pl.pallas_call(
        functools.partial(_moe_grouped_kernel, tm=tm),
        grid=(n // tm,),
        in_specs=[pl.BlockSpec((tm, D_MODEL), lambda i: (i, 0))] + [_const_spec(c.shape) for c in consts],
        out_specs=pl.BlockSpec((tm, D_MODEL), lambda i: (i, 0)),
        out_shape=jax.ShapeDtypeStruct((n, D_MODEL), F32),
        scratch_shapes=[pltpu.VMEM((tm, D_MODEL), BF16),
                        pltpu.VMEM((2 * SUBLANES, tm), BF16),
                        pltpu.VMEM((SUBLANES, tm), F32),
                        pltpu.VMEM((tm, D_MODEL), F32)],
        compiler_params=pltpu.CompilerParams(dimension_semantics=("arbitrary",), vmem_limit_bytes=VMEM_LIMIT),
        name="moe_grouped",
    )(x, *consts)


def _layer_params(l, lbs, w_in, b_in, conv_w, conv_b, ml_norm_g, hg_norm_g, w_out, ln1_g, ln1_b,
                  w_router_group, b_router_group, w_router_expert, b_router_expert,
                  w_gate, w_up, w_down, ln2_g, ln2_b):
    g0 = 4 * ML_WIDTH
    w = w_in[l]
    b = b_in[l]
    w_main = jnp.concatenate([w[:, :g0], w[:, ML_COLS:]], axis=1).astype(BF16)
    b_main = jnp.concatenate([b[:g0], b[ML_COLS:]])[None, :]
    zw = jnp.zeros((D_MODEL, LANES - ML_HEADS), F32)
    zb = jnp.zeros((LANES - ML_HEADS,), F32)
    w_gate_cols = jnp.concatenate([w[:, g0:g0 + ML_HEADS], zw, w[:, g0 + ML_HEADS:ML_COLS], zw], axis=1).astype(BF16)
    b_gate_cols = jnp.concatenate([b[g0:g0 + ML_HEADS], zb, b[g0 + ML_HEADS:ML_COLS], zb])[None, :]
    pad_r = LANES - N_GROUPS - N_EXPERTS
    w_router = jnp.concatenate([w_router_group[l], w_router_expert[l], jnp.zeros((D_MODEL, pad_r), F32)],
                               axis=1).astype(BF16)
    b_router = jnp.concatenate([b_router_group[l], b_router_expert[l], jnp.zeros((pad_r,), F32)])[None, :]
    w_router_t = jnp.concatenate([w_router_expert[l], w_router_group[l], jnp.zeros((D_MODEL, pad_r), F32)],
                                 axis=1).T.astype(BF16)
    b_router_t = jnp.concatenate([b_router_expert[l], b_router_group[l], jnp.zeros((pad_r,), F32)])[:, None]
    lb = lbs[l]
    return {
        "w_router_t": w_router_t, "b_router_t": b_router_t,
        "w_main": w_main, "b_main": b_main, "w_gate": w_gate_cols, "b_gate": b_gate_cols,
        "conv_w": conv_w[l], "conv_b": conv_b[l][None, :], "ml_g": ml_norm_g[l][None, :],
        "hg_g": hg_norm_g[l][None, :], "log_lb": jnp.log(lb)[None, :], "log1m_lb": jnp.log1p(-lb)[None, :],
        "one_m_lb": (1.0 - lb)[None, :], "w_out": w_out[l].astype(BF16),
        "ln1_g": ln1_g[l][None, :], "ln1_b": ln1_b[l][None, :],
        "w_router": w_router, "b_router": b_router,
        "w_gate_e": w_gate[l].astype(BF16), "w_up_e": w_up[l].astype(BF16),
        "w_down_e": w_down[l].astype(BF16).reshape(N_EXPERTS * EXPERT_FF, D_MODEL),
        "ln2_g": ln2_g[l][None, :], "ln2_b": ln2_b[l][None, :],
    }


def _pick_tile(n, pref):
    t = pref
    while n % t:
        t //= 2
    return t


def kernel(x_prompt, x_sample, state_conv, state_ml_C, state_ml_n, state_ml_m, state_hg_S, w_in, b_in, conv_w, conv_b, ml_norm_g, hg_lower_bounds, hg_norm_g, w_out, ln1_g, ln1_b, w_router_group, b_router_group, w_router_expert, b_router_expert, w_gate, w_up, w_down, ln2_g, ln2_b):
    sm = jax.nn.softmax(hg_lower_bounds.astype(F32), axis=0)
    lbs = jnp.concatenate([jnp.zeros_like(sm[:1]), jnp.cumsum(sm[1:], axis=0)], axis=0)
    bsz, seq, _ = x_prompt.shape
    nb = x_sample.shape[0]
    tt = _pick_tile(seq, 256)
    y_p = x_prompt
    y_s = x_sample.reshape(nb, D_MODEL)
    n_tok = bsz * seq
    conv_in = state_conv.reshape(DEPTH, nb, -1)
    n_in = state_ml_n.reshape(DEPTH, nb, ML_WIDTH)
    st_p = st_s = None
    for l in range(DEPTH):
        p = _layer_params(l, lbs, w_in, b_in, conv_w, conv_b, ml_norm_g, hg_norm_g, w_out, ln1_g, ln1_b,
                          w_router_group, b_router_group, w_router_expert, b_router_expert,
                          w_gate, w_up, w_down, ln2_g, ln2_b)
        x1, *st_p = _prompt_mixer(y_p, p, tt, l, st_p)
        y_p = _moe_grouped(x1.reshape(n_tok, D_MODEL), p, _pick_tile(n_tok, 512)).reshape(bsz, seq, D_MODEL)
        x1s, *st_s = _decode_mixer(y_s, conv_in, state_ml_C, n_in, state_ml_m, state_hg_S, p, l, st_s)
        y_s = _moe(x1s, p, _pick_tile(nb, 256))
    conv_p, caug, m_p, s_p = st_p
    conv_s, c_s, n_s, m_s, s_s = st_s
    return (y_p, y_s.reshape(nb, 1, D_MODEL),
            conv_p, caug[..., :ML_DK], caug[..., ML_DK], m_p[:, :, :ML_HEADS, 0], s_p,
            conv_s.reshape(DEPTH, nb, CONV_WIDTH - 1, 2 * ML_WIDTH), c_s,
            n_s.reshape(DEPTH, nb, ML_HEADS, ML_DK), m_s, s_s)
```

```python
import functools

import jax
import jax.numpy as jnp
from jax import lax
from jax.experimental import pallas as pl
from jax.experimental.pallas import tpu as pltpu

F32 = jnp.float32
BF16 = jnp.bfloat16

D_MODEL = 1024
DEPTH = 2
ML_HEADS = 4
ML_WIDTH = 512
ML_DK = 128
ML_CHUNK = 128
CONV_WIDTH = 4
HG_HEADS = 4
HG_WIDTH = 512
HG_DK = 128
HG_CHUNK = 64
ML_COLS = 4 * ML_WIDTH + 2 * ML_HEADS
N_GROUPS = 4
EXPERTS_PER_GROUP = 4
N_EXPERTS = 16
EXPERT_FF = 256
DEEPNORM_ALPHA = (2 * DEPTH) ** 0.25
LN_EPS = 1e-5
RMS_EPS = 1e-6

LANES = 128
SUBLANES = 8
MAIN_COLS = 4 * ML_WIDTH + 4 * HG_WIDTH
GATE_COLS = 2 * LANES
CONV_PAD = SUBLANES
HG_SAFE_DECAY = 150.0
VMEM_LIMIT = 56 * 1024 * 1024


def _dot(a, b):
    return jnp.dot(a, b, preferred_element_type=F32)


def _dot_nt(a, b):
    return lax.dot_general(a, b, (((1,), (1,)), ((), ())), preferred_element_type=F32)


def _dot_tn(a, b):
    return lax.dot_general(a, b, (((0,), (0,)), ((), ())), preferred_element_type=F32)


def _sigmoid(x):
    return 1.0 / (1.0 + jnp.exp(-x))


def _silu(x):
    return x * _sigmoid(x)


def _log_sigmoid(x):
    return jnp.minimum(x, 0.0) - jnp.log(1.0 + jnp.exp(-jnp.abs(x)))


def _cumsum_rows(tri_bf, x):
    hi = x.astype(BF16)
    lo = (x - hi.astype(F32)).astype(BF16)
    return _dot(tri_bf, hi) + _dot(tri_bf, lo)


def _layer_norm_rows(x, g, b):
    mu = jnp.mean(x, axis=-1, keepdims=True)
    xc = x - mu
    var = jnp.mean(xc * xc, axis=-1, keepdims=True)
    return xc * lax.rsqrt(var + LN_EPS) * g + b


def _head_rms(h, g):
    return h * lax.rsqrt(jnp.mean(h * h, axis=-1, keepdims=True) + RMS_EPS) * g


def _hg_gates(hf, log_lb, log1m_lb, one_m_lb):
    e = jnp.exp(-jnp.abs(hf))
    one_p_e = 1.0 + e
    log_sig = jnp.minimum(hf, 0.0) - jnp.log(one_p_e)
    b = log1m_lb + log_sig
    hi = jnp.maximum(log_lb, b)
    lo = jnp.minimum(log_lb, b)
    lf = hi + jnp.log(1.0 + jnp.exp(lo - hi))
    k = one_m_lb * jnp.where(hf >= 0.0, e, 1.0) / one_p_e
    return lf, k


class _TileScratch:
    N_REFS = 10

    def __init__(self, z, zg, hq, hk, hi, hgc, ho, hinter, hgate, mixed):
        self.z, self.zg, self.hq, self.hk, self.hi, self.hgc = z, zg, hq, hk, hi, hgc
        self.ho, self.hinter, self.hgate, self.mixed = ho, hinter, hgate, mixed


def _prompt_mixer_pair_kernel(x_ref, xn_ref, w_main_ref, w_gate_ref, b_main_ref, b_gate_ref, conv_w_ref, conv_b_ref,
                              ml_g_ref, hg_g_ref, log_lb_ref, log1m_lb_ref, one_m_lb_ref, w_out_ref,
                              ln_g_ref, ln_b_ref,
                              x1_ref, conv_out_ref, caug_out_ref, m_out_ref, s_out_ref,
                              xp_scr, caug_scr, m_scr, st_scr, *tile_scr, tt):
    t = pl.program_id(1)
    nt = pl.num_programs(1)
    n_refs = _TileScratch.N_REFS
    tiles = (_TileScratch(*tile_scr[0:n_refs]), _TileScratch(*tile_scr[n_refs:2 * n_refs]))

    @pl.when(t == 0)
    def _():
        xp_scr[0:CONV_PAD, :] = jnp.zeros((CONV_PAD, 2 * ML_WIDTH), F32)
        caug_scr[...] = jnp.zeros_like(caug_scr)
        m_scr[...] = jnp.zeros_like(m_scr)
        st_scr[...] = jnp.zeros_like(st_scr)

    piece_cols = 2 * LANES
    sections = [(lo, lo + piece_cols) for lo in range(0, MAIN_COLS, piece_cols)]

    def projection_pieces(src_bf, dst):
        def main(lo, hi):
            dst.z[:, lo:hi] = _dot(src_bf, w_main_ref[:, lo:hi])

        def gates():
            dst.zg[...] = _dot(src_bf, w_gate_ref[...])

        return [functools.partial(main, lo, hi) for lo, hi in sections] + [gates]

    @pl.when((pl.program_id(0) == 0) & (t == 0))
    def _():
        for piece in projection_pieces(x_ref[0, 0:tt, :].astype(BF16), tiles[0]):
            piece()

    row = lax.broadcasted_iota(jnp.int32, (ML_CHUNK, ML_CHUNK), 0)
    col = lax.broadcasted_iota(jnp.int32, (ML_CHUNK, ML_CHUNK), 1)
    tril = row >= col
    tri_bf = jnp.where(tril, 1.0, 0.0).astype(BF16)
    ones_col = jnp.where(lax.broadcasted_iota(jnp.int32, (ML_CHUNK, LANES), 1) == 0, 1.0, 0.0).astype(BF16)
    tril64 = tril[:HG_CHUNK, :HG_CHUNK]
    row64 = row[:HG_CHUNK, :1]
    rt = lax.broadcasted_iota(jnp.int32, (tt, tt), 0)
    ct = lax.broadcasted_iota(jnp.int32, (tt, tt), 1)
    blk_tri = jnp.where((rt >= ct) & ((rt >> 6) == (ct >> 6)), 1.0, 0.0).astype(BF16)
    base = 4 * ML_WIDTH

    def finish(ts, rows_out):
        for h in range(HG_HEADS):
            hs = slice(h * HG_DK, (h + 1) * HG_DK)
            o = _head_rms(ts.ho[:, hs], hg_g_ref[:, hs]) * ts.hgate[:, hs]
            ts.mixed[:, ML_WIDTH + h * HG_DK:ML_WIDTH + (h + 1) * HG_DK] = o.astype(BF16)
        mix = _dot(ts.mixed[...], w_out_ref[...])
        x1_ref[0, rows_out, :] = _layer_norm_rows(DEEPNORM_ALPHA * x_ref[0, rows_out, :] + mix,
                                                  ln_g_ref[...], ln_b_ref[...])

    def recur(ts, rows_out, fillers):
        def fill():
            if fillers:
                fillers.pop(0)()

        def proj(lo, hi):
            return ts.z[:, lo:hi] + b_main_ref[:, lo:hi]

        xp_scr[CONV_PAD:CONV_PAD + tt, :] = proj(0, 2 * ML_WIDTH)
        fill()
        qk = conv_b_ref[...] + xp_scr[CONV_PAD - 3:CONV_PAD - 3 + tt, :] * conv_w_ref[0:1, :]
        for j in range(1, CONV_WIDTH):
            qk = qk + xp_scr[CONV_PAD - 3 + j:CONV_PAD - 3 + j + tt, :] * conv_w_ref[j:j + 1, :]
        qk = _silu(qk)
        xp_scr[CONV_PAD - 3:CONV_PAD, :] = xp_scr[CONV_PAD + tt - 3:CONV_PAD + tt, :]
        q_all = qk[:, :ML_WIDTH].astype(BF16)
        k_all = qk[:, ML_WIDTH:] * (ML_DK ** -0.5)
        v_all = proj(2 * ML_WIDTH, 3 * ML_WIDTH).astype(BF16)
        o_gate = _sigmoid(proj(3 * ML_WIDTH, 4 * ML_WIDTH))
        zg = ts.zg[...] + b_gate_ref[...]
        fill()

        for c in range(tt // ML_CHUNK):
            r0 = c * ML_CHUNK
            li_s = zg[r0:r0 + ML_CHUNK, 0:LANES]
            lf_s = _log_sigmoid(zg[r0:r0 + ML_CHUNK, LANES:2 * LANES])
            f_s = _cumsum_rows(tri_bf, lf_s)
            a_s = li_s - f_s
            a_t = a_s.T
            for h in range(ML_HEADS):
                hs = slice(h * ML_DK, (h + 1) * ML_DK)
                q = q_all[r0:r0 + ML_CHUNK, hs]
                k = k_all[r0:r0 + ML_CHUNK, hs]
                vaug = jnp.concatenate([v_all[r0:r0 + ML_CHUNK, hs], ones_col], axis=1)
                f_c = f_s[:, h:h + 1]
                a_c = a_s[:, h:h + 1]
                a_r = a_t[h:h + 1, :]
                m_prev = m_scr[h:h + 1, 0:1]
                cmax = jnp.max(jnp.where(tril, a_r, -jnp.inf), axis=1, keepdims=True)
                m_t = f_c + jnp.maximum(m_prev, cmax)
                dmat = jnp.where(tril, jnp.exp((f_c - m_t) + a_r), 0.0)
                s = _dot_nt(q, k.astype(BF16)) * dmat
                dec = jnp.exp(f_c + m_prev - m_t)
                caug = caug_scr[h]
                tot = _dot(s.astype(BF16), vaug) + dec * _dot(q, caug.astype(BF16))
                num = tot[:, :ML_DK]
                qn = tot[:, ML_DK:ML_DK + 1]
                hout = num / jnp.maximum(jnp.abs(qn), jnp.exp(-m_t))
                f_end = f_c[ML_CHUNK - 1:ML_CHUNK, :]
                m_end = m_t[ML_CHUNK - 1:ML_CHUNK, :]
                w_end = jnp.exp(f_end + a_c - m_end)
                dec_end = jnp.exp(f_end + m_prev - m_end)
                caug_scr[h] = dec_end * caug + _dot_tn((k * w_end).astype(BF16), vaug)
                m_scr[h:h + 1, :] = jnp.broadcast_to(m_end, (1, LANES))
                hout = _head_rms(o_gate[r0:r0 + ML_CHUNK, hs] * hout, ml_g_ref[:, hs])
                ts.mixed[r0:r0 + ML_CHUNK, hs] = hout.astype(BF16)
                fill()

        ts.hq[...] = _silu(proj(base, base + HG_WIDTH))
        lf_h, k_h = _hg_gates(proj(base + HG_WIDTH, base + 2 * HG_WIDTH),
                              log_lb_ref[...], log1m_lb_ref[...], one_m_lb_ref[...])
        ts.hk[...] = k_h
        fill()
        ts.hi[...] = proj(base + 2 * HG_WIDTH, base + 3 * HG_WIDTH)
        ts.hgate[...] = _silu(proj(base + 3 * HG_WIDTH, base + 4 * HG_WIDTH))
        ts.hgc[...] = _cumsum_rows(blk_tri, lf_h)
        g_min = None
        for c in range(tt // HG_CHUNK):
            r0 = c * HG_CHUNK
            rows = slice(r0, r0 + HG_CHUNK)
            for h in range(HG_HEADS):
                hs = slice(h * HG_DK, (h + 1) * HG_DK)
                g = ts.hgc[rows, hs]
                q = ts.hq[rows, hs]
                k = ts.hk[rows, hs]
                iv = ts.hi[rows, hs].astype(BF16)
                g_end = g[HG_CHUNK - 1:HG_CHUNK, :]
                g_min = g_end if g_min is None else jnp.minimum(g_min, g_end)
                st = st_scr[h]
                inter = _dot_nt((q * jnp.exp(g)).astype(BF16), st.astype(BF16))
                ref = 0.5 * g_end
                qa = (q * jnp.exp(g - ref)).astype(BF16)
                kb = (k * jnp.exp(ref - g)).astype(BF16)
                amat = jnp.where(tril64, _dot_nt(qa, kb), 0.0)
                ts.hinter[rows, hs] = inter
                ts.ho[rows, hs] = _dot(amat.astype(BF16), iv) + inter
                st_scr[h] = jnp.exp(g_end) * st + _dot_tn(iv, (k * jnp.exp(g_end - g)).astype(BF16))
                if h % 2 == 1:
                    fill()
        while fillers:
            fill()
        finish(ts, rows_out)
        return g_min

    def redo_exact(ts, rows_out):
        for c in range(tt // HG_CHUNK):
            r0 = c * HG_CHUNK
            rows = slice(r0, r0 + HG_CHUNK)
            for h in range(HG_HEADS):
                hs = slice(h * HG_DK, (h + 1) * HG_DK)
                g = ts.hgc[rows, hs]
                k = ts.hk[rows, hs]
                ivf = ts.hi[rows, hs]

                def body(grp, carry):
                    rows8 = pl.ds(pl.multiple_of(r0 + grp * SUBLANES, SUBLANES), SUBLANES)
                    g8 = ts.hgc[rows8, hs]
                    q8 = ts.hq[rows8, hs]
                    outs = []
                    for jj in range(SUBLANES):
                        w = jnp.exp(jnp.minimum(g8[jj:jj + 1, :] - g, 0.0)) * k * q8[jj:jj + 1, :]
                        a_col = jnp.sum(w, axis=1, keepdims=True)
                        a_col = jnp.where(row64 <= grp * SUBLANES + jj, a_col, 0.0)
                        outs.append(jnp.sum(a_col * ivf, axis=0, keepdims=True))
                    ts.ho[rows8, hs] = jnp.concatenate(outs, axis=0) + ts.hinter[rows8, hs]
                    return carry

                lax.fori_loop(0, HG_CHUNK // SUBLANES, body, 0)
        finish(ts, rows_out)

    rows_a, rows_b = slice(0, tt), slice(tt, 2 * tt)
    g_min_a = recur(tiles[0], rows_a, projection_pieces(x_ref[0, rows_b, :].astype(BF16), tiles[1]))
    g_min_b = recur(tiles[1], rows_b, projection_pieces(xn_ref[0].astype(BF16), tiles[0]))

    @pl.when(jnp.min(jnp.minimum(g_min_a, g_min_b)) <= -HG_SAFE_DECAY)
    def _():
        redo_exact(tiles[0], rows_a)
        redo_exact(tiles[1], rows_b)

    @pl.when(t == nt - 1)
    def _():
        for h in range(ML_HEADS):
            caug_out_ref[0, h] = caug_scr[h]
            s_out_ref[0, h] = st_scr[h].T
        m_out_ref[0] = m_scr[...]
        conv_out_ref[0] = xp_scr[CONV_PAD - 3:CONV_PAD, :]


def _const_spec(shape):
    nd = len(shape)
    return pl.BlockSpec(shape, lambda *_: (0,) * nd, pipeline_mode=pl.Buffered(1))


def _skip_alias_refs(body, n_in, n_alias, *refs):
    return body(*refs[:n_in], *refs[n_in + n_alias:])


def _alias_args(prev, n_in):
    if prev is None:
        return [], (), {}
    specs = [pl.BlockSpec(memory_space=pl.ANY)] * len(prev)
    return specs, tuple(prev), {n_in + k: 1 + k for k in range(len(prev))}


def _prompt_mixer(x, p, tt, layer, prev):
    bsz, seq, _ = x.shape
    consts = (p["w_main"], p["w_gate"], p["b_main"], p["b_gate"], p["conv_w"], p["conv_b"], p["ml_g"],
              p["hg_g"], p["log_lb"], p["log1m_lb"], p["one_m_lb"], p["w_out"], p["ln1_g"], p["ln1_b"])
    out_shape = (
        jax.ShapeDtypeStruct((bsz, seq, D_MODEL), F32),
        jax.ShapeDtypeStruct((DEPTH, bsz, CONV_WIDTH - 1, 2 * ML_WIDTH), F32),
        jax.ShapeDtypeStruct((DEPTH, bsz, ML_HEADS, ML_DK, 2 * ML_DK), F32),
        jax.ShapeDtypeStruct((DEPTH, bsz, SUBLANES, LANES), F32),
        jax.ShapeDtypeStruct((DEPTH, bsz, HG_HEADS, HG_DK, HG_DK), F32),
    )
    out_specs = (
        pl.BlockSpec((1, 2 * tt, D_MODEL), lambda b, t: (b, t, 0)),
        pl.BlockSpec((None, 1, CONV_WIDTH - 1, 2 * ML_WIDTH), lambda b, t: (layer, b, 0, 0)),
        pl.BlockSpec((None, 1, ML_HEADS, ML_DK, 2 * ML_DK), lambda b, t: (layer, b, 0, 0, 0)),
        pl.BlockSpec((None, 1, SUBLANES, LANES), lambda b, t: (layer, b, 0, 0)),
        pl.BlockSpec((None, 1, HG_HEADS, HG_DK, HG_DK), lambda b, t: (layer, b, 0, 0, 0)),
    )
    n_in = 2 + len(consts)
    alias_specs, alias_ops, alias_map = _alias_args(prev, n_in)
    tile_scratch = [
        pltpu.VMEM((tt, MAIN_COLS), F32),
        pltpu.VMEM((tt, GATE_COLS), F32),
        pltpu.VMEM((tt, HG_WIDTH), F32),
        pltpu.VMEM((tt, HG_WIDTH), F32),
        pltpu.VMEM((tt, HG_WIDTH), F32),
        pltpu.VMEM((tt, HG_WIDTH), F32),
        pltpu.VMEM((tt, HG_WIDTH), F32),
        pltpu.VMEM((tt, HG_WIDTH), F32),
        pltpu.VMEM((tt, HG_WIDTH), F32),
        pltpu.VMEM((tt, 2 * ML_WIDTH), BF16),
    ]
    assert len(tile_scratch) == _TileScratch.N_REFS
    scratch = [
        pltpu.VMEM((CONV_PAD + tt, 2 * ML_WIDTH), F32),
        pltpu.VMEM((ML_HEADS, ML_DK, 2 * ML_DK), F32),
        pltpu.VMEM((SUBLANES, LANES), F32),
        pltpu.VMEM((HG_HEADS, HG_DK, HG_DK), F32),
    ] + tile_scratch + tile_scratch
    n_pairs = seq // (2 * tt)
    last_pair = bsz * n_pairs - 1

    def next_first_tile(b, t):
        nxt = jnp.minimum(b * n_pairs + t + 1, last_pair)
        return (nxt // n_pairs, 2 * (nxt % n_pairs), 0)

    return pl.pallas_call(
        functools.partial(_skip_alias_refs, functools.partial(_prompt_mixer_pair_kernel, tt=tt), n_in,
                          len(alias_ops)),
        grid=(bsz, n_pairs),
        in_specs=([pl.BlockSpec((1, 2 * tt, D_MODEL), lambda b, t: (b, t, 0)),
                   pl.BlockSpec((1, tt, D_MODEL), next_first_tile)]
                  + [_const_spec(c.shape) for c in consts] + alias_specs),
        out_specs=out_specs,
        out_shape=out_shape,
        scratch_shapes=scratch,
        input_output_aliases=alias_map,
        compiler_params=pltpu.CompilerParams(dimension_semantics=("arbitrary", "arbitrary"),
                                             vmem_limit_bytes=VMEM_LIMIT),
        name="prompt_mixer",
    )(x, x, *consts, *alias_ops)


DEC_BLOCK = SUBLANES


def _decode_mixer_kernel(x_ref, conv_ref, c_ref, n_ref, m_ref, s_ref,
                         w_main_ref, w_gate_ref, b_main_ref, b_gate_ref, conv_w_ref, conv_b_ref,
                         ml_g_ref, hg_g_ref, log_lb_ref, log1m_lb_ref, one_m_lb_ref, w_out_ref,
                         ln_g_ref, ln_b_ref,
                         x1_ref, conv_out_ref, c_out_ref, n_out_ref, m_out_ref, s_out_ref,
                         q_scr, k_scr, v_scr, og_scr, li_scr, lf_scr, hq_scr, hk_scr, hf_scr, hi_scr, hgate_scr,
                         qt_scr, kt_scr, hqt_scr, hkt_scr, hft_scr, hml_scr, ho_scr, *, nb):
    i = pl.program_id(0)
    nsteps = pl.num_programs(0)

    @pl.when(i == 0)
    def _():
        xb = x_ref[...].astype(BF16)

        def proj(lo, hi):
            return _dot(xb, w_main_ref[:, lo:hi]) + b_main_ref[:, lo:hi]

        qk_raw = proj(0, 2 * ML_WIDTH)
        w = 2 * ML_WIDTH
        qk = conv_b_ref[...] + qk_raw * conv_w_ref[CONV_WIDTH - 1:CONV_WIDTH, :]
        for j in range(CONV_WIDTH - 1):
            qk = qk + conv_ref[:, j * w:(j + 1) * w] * conv_w_ref[j:j + 1, :]
        qk = _silu(qk)
        conv_out_ref[:, 0:2 * w] = conv_ref[:, w:3 * w]
        conv_out_ref[:, 2 * w:3 * w] = qk_raw
        q_scr[...] = qk[:, :ML_WIDTH]
        k_scr[...] = qk[:, ML_WIDTH:] * (ML_DK ** -0.5)
        v_scr[...] = proj(2 * ML_WIDTH, 3 * ML_WIDTH)
        og_scr[...] = _sigmoid(proj(3 * ML_WIDTH, 4 * ML_WIDTH))
        zg = _dot(xb, w_gate_ref[...]) + b_gate_ref[...]
        li_scr[...] = zg[:, 0:LANES]
        lf_scr[...] = _log_sigmoid(zg[:, LANES:2 * LANES])
        base = 4 * ML_WIDTH
        hq_scr[...] = _silu(proj(base, base + HG_WIDTH))
        lf_h, k_h = _hg_gates(proj(base + HG_WIDTH, base + 2 * HG_WIDTH),
                              log_lb_ref[...], log1m_lb_ref[...], one_m_lb_ref[...])
        hk_scr[...] = k_h
        hf_scr[...] = jnp.exp(lf_h)
        hi_scr[...] = proj(base + 2 * HG_WIDTH, base + 3 * HG_WIDTH)
        hgate_scr[...] = _silu(proj(base + 3 * HG_WIDTH, base + 4 * HG_WIDTH))
        for h in range(ML_HEADS):
            hs = slice(h * ML_DK, (h + 1) * ML_DK)
            qt_scr[h] = q_scr[:, hs].T
            kt_scr[h] = k_scr[:, hs].T
            hqt_scr[h] = hq_scr[:, hs].T
            hkt_scr[h] = hk_scr[:, hs].T
            hft_scr[h] = hf_scr[:, hs].T

    b0 = pl.multiple_of(i * DEC_BLOCK, DEC_BLOCK)
    shift = (nb - b0) & (nb - 1)
    rows = pl.ds(b0, DEC_BLOCK)
    li = li_scr[rows, :]
    lf = lf_scr[rows, :]
    m_prev = m_ref[...]
    q_blk = q_scr[rows, :]
    k_blk = k_scr[rows, :]
    v_blk = v_scr[rows, :]
    hi_blk = hi_scr[rows, :]
    n_blk = n_ref[...]
    m_new_cols = []
    for h in range(ML_HEADS):
        hs = slice(h * ML_DK, (h + 1) * ML_DK)
        li_h = li[:, h:h + 1]
        lf_h = lf[:, h:h + 1]
        mp_h = m_prev[:, h:h + 1]
        m_t = jnp.maximum(lf_h + mp_h, li_h)
        dmat = jnp.exp(li_h - m_t)
        dec = jnp.exp(lf_h + mp_h - m_t)
        n_h = n_blk[:, hs]
        s = jnp.sum(q_blk[:, hs] * k_blk[:, hs], axis=1, keepdims=True) * dmat
        qn = s + dec * jnp.sum(q_blk[:, hs] * n_h, axis=1, keepdims=True)
        denom = jnp.maximum(jnp.abs(qn), jnp.exp(-m_t))
        n_out_ref[:, hs] = dec * n_h + dmat * k_blk[:, hs]
        m_new_cols.append(m_t)
        qt = pltpu.roll(qt_scr[h], shift, 1)
        kt = pltpu.roll(kt_scr[h], shift, 1)
        hqt = pltpu.roll(hqt_scr[h], shift, 1)
        hkt = pltpu.roll(hkt_scr[h], shift, 1)
        hft = pltpu.roll(hft_scr[h], shift, 1)
        qc_rows, ho_rows = [], []
        for j in range(DEC_BLOCK):
            jr = slice(j, j + 1)
            cmat = c_ref[j, h]
            qc_rows.append(jnp.sum(qt[:, jr] * cmat, axis=0, keepdims=True))
            c_out_ref[j, h] = dec[jr, :] * cmat + (dmat[jr, :] * kt[:, jr]) * v_blk[jr, hs]
            s_new = hft[:, jr] * s_ref[j, h] + hkt[:, jr] * hi_blk[jr, hs]
            s_out_ref[j, h] = s_new
            ho_rows.append(jnp.sum(hqt[:, jr] * s_new, axis=0, keepdims=True))
        num = s * v_blk[:, hs] + dec * jnp.concatenate(qc_rows, axis=0)
        hml_scr[rows, hs] = num / denom
        ho_scr[rows, hs] = jnp.concatenate(ho_rows, axis=0)
    m_out_ref[...] = jnp.concatenate(m_new_cols, axis=1)

    @pl.when(i == nsteps - 1)
    def _():
        for h in range(ML_HEADS):
            hs = slice(h * ML_DK, (h + 1) * ML_DK)
            hm = _head_rms(og_scr[:, hs] * hml_scr[:, hs], ml_g_ref[:, hs])
            hml_scr[:, hs] = hm
            ho = _head_rms(ho_scr[:, hs], hg_g_ref[:, hs]) * hgate_scr[:, hs]
            ho_scr[:, hs] = ho
        mix = (_dot(hml_scr[...].astype(BF16), w_out_ref[0:ML_WIDTH, :])
               + _dot(ho_scr[...].astype(BF16), w_out_ref[ML_WIDTH:2 * ML_WIDTH, :]))
        x1_ref[...] = _layer_norm_rows(DEEPNORM_ALPHA * x_ref[...] + mix, ln_g_ref[...], ln_b_ref[...])


def _decode_mixer(x, conv, c_state, n_state, m_state, s_state, p, layer, prev):
    nb = x.shape[0]
    assert nb == LANES, "decode batch must fill one lane tile"
    consts = (p["w_main"], p["w_gate"], p["b_main"], p["b_gate"], p["conv_w"], p["conv_b"], p["ml_g"],
              p["hg_g"], p["log_lb"], p["log1m_lb"], p["one_m_lb"], p["w_out"], p["ln1_g"], p["ln1_b"])
    conv_cols = (CONV_WIDTH - 1) * 2 * ML_WIDTH
    blk4 = lambda i: (layer, i, 0, 0, 0)
    blk2 = lambda i: (layer, i, 0)
    conv_spec = pl.BlockSpec((None, nb, conv_cols), lambda i: (layer, 0, 0))
    state_specs = [
        pl.BlockSpec((None, DEC_BLOCK, ML_HEADS, ML_DK, ML_DK), blk4),
        pl.BlockSpec((None, DEC_BLOCK, ML_WIDTH), blk2),
        pl.BlockSpec((None, DEC_BLOCK, ML_HEADS), blk2),
        pl.BlockSpec((None, DEC_BLOCK, HG_HEADS, HG_DK, HG_DK), blk4),
    ]
    full2 = lambda shape: pl.BlockSpec(shape, lambda i: (0, 0))
    out_shape = (
        jax.ShapeDtypeStruct((nb, D_MODEL), F32),
        jax.ShapeDtypeStruct(conv.shape, F32),
        jax.ShapeDtypeStruct(c_state.shape, F32),
        jax.ShapeDtypeStruct(n_state.shape, F32),
        jax.ShapeDtypeStruct(m_state.shape, F32),
        jax.ShapeDtypeStruct(s_state.shape, F32),
    )
    n_in = 6 + len(consts)
    alias_specs, alias_ops, alias_map = _alias_args(prev, n_in)
    rows_f32 = lambda w: pltpu.VMEM((nb, w), F32)
    scratch = ([rows_f32(ML_WIDTH)] * 4 + [rows_f32(LANES)] * 2 + [rows_f32(HG_WIDTH)] * 5
               + [pltpu.VMEM((ML_HEADS, ML_DK, nb), F32)] * 5 + [rows_f32(ML_WIDTH), rows_f32(HG_WIDTH)])
    return pl.pallas_call(
        functools.partial(_skip_alias_refs, functools.partial(_decode_mixer_kernel, nb=nb), n_in, len(alias_ops)),
        grid=(nb // DEC_BLOCK,),
        in_specs=([full2((nb, D_MODEL)), conv_spec] + state_specs + [_const_spec(c.shape) for c in consts]
                  + alias_specs),
        out_specs=(full2((nb, D_MODEL)), conv_spec) + tuple(state_specs),
        out_shape=out_shape,
        scratch_shapes=scratch,
        input_output_aliases=alias_map,
        compiler_params=pltpu.CompilerParams(dimension_semantics=("arbitrary",), vmem_limit_bytes=VMEM_LIMIT),
        name="decode_mixer",
    )(x, conv, c_state, n_state, m_state, s_state, *consts, *alias_ops)


def _moe_kernel(x_ref, w_r_ref, b_r_ref, w1_ref, w3_ref, w2_ref, ln_g_ref, ln_b_ref, y_ref, hdn_scr):
    x = x_ref[...]
    xb = x.astype(BF16)
    logits = _dot(xb, w_r_ref[...]) + b_r_ref[...]
    lg = logits[:, 0:N_GROUPS]
    le = logits[:, N_GROUPS:N_GROUPS + N_EXPERTS]
    gidx = lax.broadcasted_iota(jnp.int32, lg.shape, 1)
    gmax = jnp.max(lg, axis=1, keepdims=True)
    g = jnp.min(jnp.where(lg == gmax, gidx, N_GROUPS), axis=1, keepdims=True)
    pg = 1.0 / jnp.sum(jnp.exp(lg - gmax), axis=1, keepdims=True)
    eidx = lax.broadcasted_iota(jnp.int32, le.shape, 1)
    in_g = (eidx >> 2) == g
    lm = jnp.where(in_g, le, -jnp.inf)
    v1 = jnp.max(lm, axis=1, keepdims=True)
    i1 = jnp.min(jnp.where(lm == v1, eidx, N_EXPERTS), axis=1, keepdims=True)
    lm2 = jnp.where(eidx == i1, -jnp.inf, lm)
    v2 = jnp.max(lm2, axis=1, keepdims=True)
    i2 = jnp.min(jnp.where(lm2 == v2, eidx, N_EXPERTS), axis=1, keepdims=True)
    e21 = jnp.exp(v2 - v1)
    wt1 = pg / (1.0 + e21)
    wt2 = wt1 * e21
    gate = jnp.where(eidx == i1, wt1, 0.0) + jnp.where(eidx == i2, wt2, 0.0)
    for e in range(N_EXPERTS):
        a = _dot(xb, w1_ref[e])
        u = _dot(xb, w3_ref[e])
        hdn = _silu(a) * u * gate[:, e:e + 1]
        hdn_scr[:, e * EXPERT_FF:(e + 1) * EXPERT_FF] = hdn.astype(BF16)
    moe = _dot(hdn_scr[...], w2_ref[...])
    y_ref[...] = _layer_norm_rows(DEEPNORM_ALPHA * x + moe, ln_g_ref[...], ln_b_ref[...])


def _moe(x, p, tm):
    n = x.shape[0]
    consts = (p["w_router"], p["b_router"], p["w_gate_e"], p["w_up_e"], p["w_down_e"], p["ln2_g"], p["ln2_b"])
    return pl.pallas_call(
        _moe_kernel,
        grid=(n // tm,),
        in_specs=[pl.BlockSpec((tm, D_MODEL), lambda i: (i, 0))] + [_const_spec(c.shape) for c in consts],
        out_specs=pl.BlockSpec((tm, D_MODEL), lambda i: (i, 0)),
        out_shape=jax.ShapeDtypeStruct((n, D_MODEL), F32),
        scratch_shapes=[pltpu.VMEM((tm, N_EXPERTS * EXPERT_FF), BF16)],
        compiler_params=pltpu.CompilerParams(dimension_semantics=("arbitrary",), vmem_limit_bytes=VMEM_LIMIT),
        name="moe",
    )(x, *consts)


MOE_BLOCK = 128
GROUP_FF = EXPERTS_PER_GROUP * EXPERT_FF


def _moe_grouped_kernel(x_ref, w_rt_ref, b_rt_ref, w1_ref, w3_ref, w2_ref, ln_g_ref, ln_b_ref, y_ref,
                        xb_scr, gt_scr, pos_scr, acc_scr, *, tm):
    x = x_ref[...]
    xb = x.astype(BF16)
    xb_scr[...] = xb
    logits = _dot_nt(w_rt_ref[...], xb) + b_rt_ref[...]
    ridx8 = lax.broadcasted_iota(jnp.int32, (SUBLANES, tm), 0)
    lg = jnp.where(ridx8 < N_GROUPS, logits[N_EXPERTS:N_EXPERTS + SUBLANES, :], -jnp.inf)
    gmax = jnp.max(lg, axis=0, keepdims=True)
    g = jnp.min(jnp.where(lg == gmax, ridx8, SUBLANES), axis=0, keepdims=True)
    pg = 1.0 / jnp.sum(jnp.exp(lg - gmax), axis=0, keepdims=True)
    le = logits[0:N_EXPERTS, :]
    eidx = lax.broadcasted_iota(jnp.int32, (N_EXPERTS, tm), 0)
    lm = jnp.where((eidx >> 2) == g, le, -jnp.inf)
    v1 = jnp.max(lm, axis=0, keepdims=True)
    i1 = jnp.min(jnp.where(lm == v1, eidx, N_EXPERTS), axis=0, keepdims=True)
    lm2 = jnp.where(eidx == i1, -jnp.inf, lm)
    v2 = jnp.max(lm2, axis=0, keepdims=True)
    i2 = jnp.min(jnp.where(lm2 == v2, eidx, N_EXPERTS), axis=0, keepdims=True)
    e21 = jnp.exp(v2 - v1)
    wt1 = pg / (1.0 + e21)
    wt2 = wt1 * e21
    gate = jnp.where(eidx == i1, wt1, 0.0) + jnp.where(eidx == i2, wt2, 0.0)
    gate4 = gate[0:4, :] + gate[4:8, :] + gate[8:12, :] + gate[12:16, :]
    g_hi = gate4.astype(BF16)
    g_lo = (gate4 - g_hi.astype(F32)).astype(BF16)
    gt_scr[...] = jnp.zeros_like(gt_scr)
    gt_scr[0:4, :] = g_hi
    gt_scr[SUBLANES:SUBLANES + 4, :] = g_lo

    onehot = jnp.where(ridx8 == g, 1.0, 0.0)
    tr = lax.broadcasted_iota(jnp.int32, (tm, tm), 0)
    tc = lax.broadcasted_iota(jnp.int32, (tm, tm), 1)
    before = jnp.where(tr < tc, 1.0, 0.0).astype(BF16)
    rank = _dot(onehot.astype(BF16), before)
    counts = jnp.sum(onehot, axis=1, keepdims=True).astype(jnp.int32)
    nblk_g = (counts + (MOE_BLOCK - 1)) >> (MOE_BLOCK.bit_length() - 1)
    s1 = nblk_g[0:1, :]
    s2 = s1 + nblk_g[1:2, :]
    s3 = s2 + nblk_g[2:3, :]
    n_blocks = s3 + nblk_g[3:4, :]
    cidx = lax.broadcasted_iota(jnp.int32, (SUBLANES, 1), 0)
    start_blk = jnp.where(cidx == 1, s1, 0) + jnp.where(cidx == 2, s2, 0) + jnp.where(cidx == 3, s3, 0)
    pos = jnp.sum(onehot * ((start_blk * MOE_BLOCK).astype(F32) + rank), axis=0, keepdims=True)
    pos_scr[...] = jnp.broadcast_to(pos, (SUBLANES, tm))
    acc_scr[...] = jnp.zeros_like(acc_scr)
    b1 = s1[0, 0]
    b2 = s2[0, 0]
    b3 = s3[0, 0]

    def block(b, carry):
        grp = jnp.where(b >= b1, 1, 0) + jnp.where(b >= b2, 1, 0) + jnp.where(b >= b3, 1, 0)
        slot = lax.broadcasted_iota(jnp.int32, (MOE_BLOCK, 1), 0) + b * MOE_BLOCK
        sel = jnp.where(pos_scr[0:1, :] == slot.astype(F32), 1.0, 0.0).astype(BF16)
        xs = _dot(sel, xb_scr[...]).astype(BF16)
        gs = _dot_nt(sel, gt_scr[...])
        gsel = gs[:, 0:4] + gs[:, SUBLANES:SUBLANES + 4]
        parts = []
        for j in range(EXPERTS_PER_GROUP):
            e = grp * EXPERTS_PER_GROUP + j
            a = _dot(xs, w1_ref[e])
            u = _dot(xs, w3_ref[e])
            parts.append((_silu(a) * u * gsel[:, j:j + 1]).astype(BF16))
        hdn = jnp.concatenate(parts, axis=1)
        ys = _dot(hdn, w2_ref[pl.ds(pl.multiple_of(grp * GROUP_FF, GROUP_FF), GROUP_FF), :])
        y_hi = ys.astype(BF16)
        y_lo = (ys - y_hi.astype(F32)).astype(BF16)
        acc_scr[...] += _dot_tn(jnp.concatenate([sel, sel], axis=0), jnp.concatenate([y_hi, y_lo], axis=0))
        return carry

    lax.fori_loop(0, n_blocks[0, 0], block, 0)
    y_ref[...] = _layer_norm_rows(DEEPNORM_ALPHA * x + acc_scr[...], ln_g_ref[...], ln_b_ref[...])


def _moe_grouped(x, p, tm):
    n = x.shape[0]
    consts = (p["w_router_t"], p["b_router_t"], p["w_gate_e"], p["w_up_e"], p["w_down_e"], p["ln2_g"], p["ln2_b"])
    return pl.pallas_call(
        functools.partial(_moe_grouped_kernel, tm=tm),
        grid=(n // tm,),
        in_specs=[pl.BlockSpec((tm, D_MODEL), lambda i: (i, 0))] + [_const_spec(c.shape) for c in consts],
        out_specs=pl.BlockSpec((tm, D_MODEL), lambda i: (i, 0)),
        out_shape=jax.ShapeDtypeStruct((n, D_MODEL), F32),
        scratch_shapes=[pltpu.VMEM((tm, D_MODEL), BF16),
                        pltpu.VMEM((2 * SUBLANES, tm), BF16),
                        pltpu.VMEM((SUBLANES, tm), F32),
                        pltpu.VMEM((tm, D_MODEL), F32)],
        compiler_params=pltpu.CompilerParams(dimension_semantics=("arbitrary",), vmem_limit_bytes=VMEM_LIMIT),
        name="moe_grouped",
    )(x, *consts)


def _layer_params(l, lbs, w_in, b_in, conv_w, conv_b, ml_norm_g, hg_norm_g, w_out, ln1_g, ln1_b,
                  w_router_group, b_router_group, w_router_expert, b_router_expert,
                  w_gate, w_up, w_down, ln2_g, ln2_b):
    g0 = 4 * ML_WIDTH
    w = w_in[l]
    b = b_in[l]
    w_main = jnp.concatenate([w[:, :g0], w[:, ML_COLS:]], axis=1).astype(BF16)
    b_main = jnp.concatenate([b[:g0], b[ML_COLS:]])[None, :]
    zw = jnp.zeros((D_MODEL, LANES - ML_HEADS), F32)
    zb = jnp.zeros((LANES - ML_HEADS,), F32)
    w_gate_cols = jnp.concatenate([w[:, g0:g0 + ML_HEADS], zw, w[:, g0 + ML_HEADS:ML_COLS], zw], axis=1).astype(BF16)
    b_gate_cols = jnp.concatenate([b[g0:g0 + ML_HEADS], zb, b[g0 + ML_HEADS:ML_COLS], zb])[None, :]
    pad_r = LANES - N_GROUPS - N_EXPERTS
    w_router = jnp.concatenate([w_router_group[l], w_router_expert[l], jnp.zeros((D_MODEL, pad_r), F32)],
                               axis=1).astype(BF16)
    b_router = jnp.concatenate([b_router_group[l], b_router_expert[l], jnp.zeros((pad_r,), F32)])[None, :]
    w_router_t = jnp.concatenate([w_router_expert[l], w_router_group[l], jnp.zeros((D_MODEL, pad_r), F32)],
                                 axis=1).T.astype(BF16)
    b_router_t = jnp.concatenate([b_router_expert[l], b_router_group[l], jnp.zeros((pad_r,), F32)])[:, None]
    lb = lbs[l]
    return {
        "w_router_t": w_router_t, "b_router_t": b_router_t,
        "w_main": w_main, "b_main": b_main, "w_gate": w_gate_cols, "b_gate": b_gate_cols,
        "conv_w": conv_w[l], "conv_b": conv_b[l][None, :], "ml_g": ml_norm_g[l][None, :],
        "hg_g": hg_norm_g[l][None, :], "log_lb": jnp.log(lb)[None, :], "log1m_lb": jnp.log1p(-lb)[None, :],
        "one_m_lb": (1.0 - lb)[None, :], "w_out": w_out[l].astype(BF16),
        "ln1_g": ln1_g[l][None, :], "ln1_b": ln1_b[l][None, :],
        "w_router": w_router, "b_router": b_router,
        "w_gate_e": w_gate[l].astype(BF16), "w_up_e": w_up[l].astype(BF16),
        "w_down_e": w_down[l].astype(BF16).reshape(N_EXPERTS * EXPERT_FF, D_MODEL),
        "ln2_g": ln2_g[l][None, :], "ln2_b": ln2_b[l][None, :],
    }


def _pick_tile(n, pref):
    t = pref
    while n % t:
        t //= 2
    return t


def kernel(x_prompt, x_sample, state_conv, state_ml_C, state_ml_n, state_ml_m, state_hg_S, w_in, b_in, conv_w, conv_b, ml_norm_g, hg_lower_bounds, hg_norm_g, w_out, ln1_g, ln1_b, w_router_group, b_router_group, w_router_expert, b_router_expert, w_gate, w_up, w_down, ln2_g, ln2_b):
    sm = jax.nn.softmax(hg_lower_bounds.astype(F32), axis=0)
    lbs = jnp.concatenate([jnp.zeros_like(sm[:1]), jnp.cumsum(sm[1:], axis=0)], axis=0)
    bsz, seq, _ = x_prompt.shape
    nb = x_sample.shape[0]
    tt = _pick_tile(seq, 256)
    y_p = x_prompt
    y_s = x_sample.reshape(nb, D_MODEL)
    n_tok = bsz * seq
    conv_in = state_conv.reshape(DEPTH, nb, -1)
    n_in = state_ml_n.reshape(DEPTH, nb, ML_WIDTH)
    st_p = st_s = None
    for l in range(DEPTH):
        p = _layer_params(l, lbs, w_in, b_in, conv_w, conv_b, ml_norm_g, hg_norm_g, w_out, ln1_g, ln1_b,
                          w_router_group, b_router_group, w_router_expert, b_router_expert,
                          w_gate, w_up, w_down, ln2_g, ln2_b)
        x1, *st_p = _prompt_mixer(y_p, p, tt, l, st_p)
        y_p = _moe_grouped(x1.reshape(n_tok, D_MODEL), p, _pick_tile(n_tok, 512)).reshape(bsz, seq, D_MODEL)
        x1s, *st_s = _decode_mixer(y_s, conv_in, state_ml_C, n_in, state_ml_m, state_hg_S, p, l, st_s)
        y_s = _moe(x1s, p, _pick_tile(nb, 256))
    conv_p, caug, m_p, s_p = st_p
    conv_s, c_s, n_s, m_s, s_s = st_s
    return (y_p, y_s.reshape(nb, 1, D_MODEL),
            conv_p, caug[..., :ML_DK], caug[..., ML_DK], m_p[:, :, :ML_HEADS, 0], s_p,
            conv_s.reshape(DEPTH, nb, CONV_WIDTH - 1, 2 * ML_WIDTH), c_s,
            n_s.reshape(DEPTH, nb, ML_HEADS, ML_DK), m_s, s_s)
```

```python
import functools

import jax
import jax.numpy as jnp
from jax import lax
from jax.experimental import pallas as pl
from jax.experimental.pallas import tpu as pltpu

F32 = jnp.float32
BF16 = jnp.bfloat16

D_MODEL = 1024
DEPTH = 2
ML_HEADS = 4
ML_WIDTH = 512
ML_DK = 128
ML_CHUNK = 128
CONV_WIDTH = 4
HG_HEADS = 4
HG_WIDTH = 512
HG_DK = 128
HG_CHUNK = 64
ML_COLS = 4 * ML_WIDTH + 2 * ML_HEADS
N_GROUPS = 4
EXPERTS_PER_GROUP = 4
N_EXPERTS = 16
EXPERT_FF = 256
DEEPNORM_ALPHA = (2 * DEPTH) ** 0.25
LN_EPS = 1e-5
RMS_EPS = 1e-6

LANES = 128
SUBLANES = 8
MAIN_COLS = 4 * ML_WIDTH + 4 * HG_WIDTH
GATE_COLS = 2 * LANES
CONV_PAD = SUBLANES
HG_SAFE_DECAY = 150.0
VMEM_LIMIT = 56 * 1024 * 1024


def _dot(a, b):
    return jnp.dot(a, b, preferred_element_type=F32)


def _dot_nt(a, b):
    return lax.dot_general(a, b, (((1,), (1,)), ((), ())), preferred_element_type=F32)


def _dot_tn(a, b):
    return lax.dot_general(a, b, (((0,), (0,)), ((), ())), preferred_element_type=F32)


def _sigmoid(x):
    return 1.0 / (1.0 + jnp.exp(-x))


def _silu(x):
    return x * _sigmoid(x)


def _log_sigmoid(x):
    return jnp.minimum(x, 0.0) - jnp.log(1.0 + jnp.exp(-jnp.abs(x)))


def _cumsum_rows(tri_bf, x):
    hi = x.astype(BF16)
    lo = (x - hi.astype(F32)).astype(BF16)
    return _dot(tri_bf, hi) + _dot(tri_bf, lo)


def _layer_norm_rows(x, g, b):
    mu = jnp.mean(x, axis=-1, keepdims=True)
    xc = x - mu
    var = jnp.mean(xc * xc, axis=-1, keepdims=True)
    return xc * lax.rsqrt(var + LN_EPS) * g + b


def _head_rms(h, g):
    return h * lax.rsqrt(jnp.mean(h * h, axis=-1, keepdims=True) + RMS_EPS) * g


def _hg_gates(hf, log_lb, log1m_lb, one_m_lb):
    e = jnp.exp(-jnp.abs(hf))
    one_p_e = 1.0 + e
    log_sig = jnp.minimum(hf, 0.0) - jnp.log(one_p_e)
    b = log1m_lb + log_sig
    hi = jnp.maximum(log_lb, b)
    lo = jnp.minimum(log_lb, b)
    lf = hi + jnp.log(1.0 + jnp.exp(lo - hi))
    k = one_m_lb * jnp.where(hf >= 0.0, e, 1.0) / one_p_e
    return lf, k


class _TileScratch:
    N_REFS = 10

    def __init__(self, z, zg, hq, hk, hi, hgc, ho, hinter, hgate, mixed):
        self.z, self.zg, self.hq, self.hk, self.hi, self.hgc = z, zg, hq, hk, hi, hgc
        self.ho, self.hinter, self.hgate, self.mixed = ho, hinter, hgate, mixed


def _prompt_mixer_pair_kernel(x_ref, xn_ref, w_main_ref, w_gate_ref, b_main_ref, b_gate_ref, conv_w_ref, conv_b_ref,
                              ml_g_ref, hg_g_ref, log_lb_ref, log1m_lb_ref, one_m_lb_ref, w_out_ref,
                              ln_g_ref, ln_b_ref,
                              x1_ref, conv_out_ref, caug_out_ref, m_out_ref, s_out_ref,
                              xp_scr, caug_scr, m_scr, st_scr, *tile_scr, tt):
    t = pl.program_id(1)
    nt = pl.num_programs(1)
    n_refs = _TileScratch.N_REFS
    tiles = (_TileScratch(*tile_scr[0:n_refs]), _TileScratch(*tile_scr[n_refs:2 * n_refs]))

    @pl.when(t == 0)
    def _():
        xp_scr[0:CONV_PAD, :] = jnp.zeros((CONV_PAD, 2 * ML_WIDTH), F32)
        caug_scr[...] = jnp.zeros_like(caug_scr)
        m_scr[...] = jnp.zeros_like(m_scr)
        st_scr[...] = jnp.zeros_like(st_scr)

    piece_cols = 2 * LANES
    sections = [(lo, lo + piece_cols) for lo in range(0, MAIN_COLS, piece_cols)]

    def projection_pieces(src_bf, dst):
        def main(lo, hi):
            dst.z[:, lo:hi] = _dot(src_bf, w_main_ref[:, lo:hi])

        def gates():
            dst.zg[...] = _dot(src_bf, w_gate_ref[...])

        return [functools.partial(main, lo, hi) for lo, hi in sections] + [gates]

    @pl.when((pl.program_id(0) == 0) & (t == 0))
    def _():
        for piece in projection_pieces(x_ref[0, 0:tt, :].astype(BF16), tiles[0]):
            piece()

    row = lax.broadcasted_iota(jnp.int32, (ML_CHUNK, ML_CHUNK), 0)
    col = lax.broadcasted_iota(jnp.int32, (ML_CHUNK, ML_CHUNK), 1)
    tril = row >= col
    tri_bf = jnp.where(tril, 1.0, 0.0).astype(BF16)
    ones_col = jnp.where(lax.broadcasted_iota(jnp.int32, (ML_CHUNK, LANES), 1) == 0, 1.0, 0.0).astype(BF16)
    tril64 = tril[:HG_CHUNK, :HG_CHUNK]
    row64 = row[:HG_CHUNK, :1]
    rt = lax.broadcasted_iota(jnp.int32, (tt, tt), 0)
    ct = lax.broadcasted_iota(jnp.int32, (tt, tt), 1)
    blk_tri = jnp.where((rt >= ct) & ((rt >> 6) == (ct >> 6)), 1.0, 0.0).astype(BF16)
    base = 4 * ML_WIDTH

    def finish(ts, rows_out):
        for h in range(HG_HEADS):
            hs = slice(h * HG_DK, (h + 1) * HG_DK)
            o = _head_rms(ts.ho[:, hs], hg_g_ref[:, hs]) * ts.hgate[:, hs]
            ts.mixed[:, ML_WIDTH + h * HG_DK:ML_WIDTH + (h + 1) * HG_DK] = o.astype(BF16)
        mix = _dot(ts.mixed[...], w_out_ref[...])
        x1_ref[0, rows_out, :] = _layer_norm_rows(DEEPNORM_ALPHA * x_ref[0, rows_out, :] + mix,
                                                  ln_g_ref[...], ln_b_ref[...])

    def recur(ts, rows_out, fillers):
        def fill():
            if fillers:
                fillers.pop(0)()

        def proj(lo, hi):
            return ts.z[:, lo:hi] + b_main_ref[:, lo:hi]

        xp_scr[CONV_PAD:CONV_PAD + tt, :] = proj(0, 2 * ML_WIDTH)
        fill()
        qk = conv_b_ref[...] + xp_scr[CONV_PAD - 3:CONV_PAD - 3 + tt, :] * conv_w_ref[0:1, :]
        for j in range(1, CONV_WIDTH):
            qk = qk + xp_scr[CONV_PAD - 3 + j:CONV_PAD - 3 + j + tt, :] * conv_w_ref[j:j + 1, :]
        qk = _silu(qk)
        xp_scr[CONV_PAD - 3:CONV_PAD, :] = xp_scr[CONV_PAD + tt - 3:CONV_PAD + tt, :]
        q_all = qk[:, :ML_WIDTH].astype(BF16)
        k_all = qk[:, ML_WIDTH:] * (ML_DK ** -0.5)
        v_all = proj(2 * ML_WIDTH, 3 * ML_WIDTH).astype(BF16)
        o_gate = _sigmoid(proj(3 * ML_WIDTH, 4 * ML_WIDTH))
        zg = ts.zg[...] + b_gate_ref[...]
        fill()

        for c in range(tt // ML_CHUNK):
            r0 = c * ML_CHUNK
            li_s = zg[r0:r0 + ML_CHUNK, 0:LANES]
            lf_s = _log_sigmoid(zg[r0:r0 + ML_CHUNK, LANES:2 * LANES])
            f_s = _cumsum_rows(tri_bf, lf_s)
            a_s = li_s - f_s
            a_t = a_s.T
            for h in range(ML_HEADS):
                hs = slice(h * ML_DK, (h + 1) * ML_DK)
                q = q_all[r0:r0 + ML_CHUNK, hs]
                k = k_all[r0:r0 + ML_CHUNK, hs]
                vaug = jnp.concatenate([v_all[r0:r0 + ML_CHUNK, hs], ones_col], axis=1)
                f_c = f_s[:, h:h + 1]
                a_c = a_s[:, h:h + 1]
                a_r = a_t[h:h + 1, :]
                m_prev = m_scr[h:h + 1, 0:1]
                cmax = jnp.max(jnp.where(tril, a_r, -jnp.inf), axis=1, keepdims=True)
                m_t = f_c + jnp.maximum(m_prev, cmax)
                dmat = jnp.where(tril, jnp.exp((f_c - m_t) + a_r), 0.0)
                s = _dot_nt(q, k.astype(BF16)) * dmat
                dec = jnp.exp(f_c + m_prev - m_t)
                caug = caug_scr[h]
                tot = _dot(s.astype(BF16), vaug) + dec * _dot(q, caug.astype(BF16))
                num = tot[:, :ML_DK]
                qn = tot[:, ML_DK:ML_DK + 1]
                hout = num / jnp.maximum(jnp.abs(qn), jnp.exp(-m_t))
                f_end = f_c[ML_CHUNK - 1:ML_CHUNK, :]
                m_end = m_t[ML_CHUNK - 1:ML_CHUNK, :]
                w_end = jnp.exp(f_end + a_c - m_end)
                dec_end = jnp.exp(f_end + m_prev - m_end)
                caug_scr[h] = dec_end * caug + _dot_tn((k * w_end).astype(BF16), vaug)
                m_scr[h:h + 1, :] = jnp.broadcast_to(m_end, (1, LANES))
                hout = _head_rms(o_gate[r0:r0 + ML_CHUNK, hs] * hout, ml_g_ref[:, hs])
                ts.mixed[r0:r0 + ML_CHUNK, hs] = hout.astype(BF16)
                fill()

        ts.hq[...] = _silu(proj(base, base + HG_WIDTH))
        lf_h, k_h = _hg_gates(proj(base + HG_WIDTH, base + 2 * HG_WIDTH),
                              log_lb_ref[...], log1m_lb_ref[...], one_m_lb_ref[...])
        ts.hk[...] = k_h
        fill()
        ts.hi[...] = proj(base + 2 * HG_WIDTH, base + 3 * HG_WIDTH)
        ts.hgate[...] = _silu(proj(base + 3 * HG_WIDTH, base + 4 * HG_WIDTH))
        ts.hgc[...] = _cumsum_rows(blk_tri, lf_h)
        g_min = None
        for c in range(tt // HG_CHUNK):
            r0 = c * HG_CHUNK
            rows = slice(r0, r0 + HG_CHUNK)
            for h in range(HG_HEADS):
                hs = slice(h * HG_DK, (h + 1) * HG_DK)
                g = ts.hgc[rows, hs]
                q = ts.hq[rows, hs]
                k = ts.hk[rows, hs]
                iv = ts.hi[rows, hs].astype(BF16)
                g_end = g[HG_CHUNK - 1:HG_CHUNK, :]
                g_min = g_end if g_min is None else jnp.minimum(g_min, g_end)
                st = st_scr[h]
                inter = _dot_nt((q * jnp.exp(g)).astype(BF16), st.astype(BF16))
                ref = 0.5 * g_end
                qa = (q * jnp.exp(g - ref)).astype(BF16)
                kb = (k * jnp.exp(ref - g)).astype(BF16)
                amat = jnp.where(tril64, _dot_nt(qa, kb), 0.0)
                ts.hinter[rows, hs] = inter
                ts.ho[rows, hs] = _dot(amat.astype(BF16), iv) + inter
                st_scr[h] = jnp.exp(g_end) * st + _dot_tn(iv, (k * jnp.exp(g_end - g)).astype(BF16))
                if h % 2 == 1:
                    fill()
        while fillers:
            fill()
        return g_min

    def redo_exact(ts, rows_out):
        for c in range(tt // HG_CHUNK):
            r0 = c * HG_CHUNK
            rows = slice(r0, r0 + HG_CHUNK)
            for h in range(HG_HEADS):
                hs = slice(h * HG_DK, (h + 1) * HG_DK)
                g = ts.hgc[rows, hs]
                k = ts.hk[rows, hs]
                ivf = ts.hi[rows, hs]

                def body(grp, carry):
                    rows8 = pl.ds(pl.multiple_of(r0 + grp * SUBLANES, SUBLANES), SUBLANES)
                    g8 = ts.hgc[rows8, hs]
                    q8 = ts.hq[rows8, hs]
                    outs = []
                    for jj in range(SUBLANES):
                        w = jnp.exp(jnp.minimum(g8[jj:jj + 1, :] - g, 0.0)) * k * q8[jj:jj + 1, :]
                        a_col = jnp.sum(w, axis=1, keepdims=True)
                        a_col = jnp.where(row64 <= grp * SUBLANES + jj, a_col, 0.0)
                        outs.append(jnp.sum(a_col * ivf, axis=0, keepdims=True))
                    ts.ho[rows8, hs] = jnp.concatenate(outs, axis=0) + ts.hinter[rows8, hs]
                    return carry

                lax.fori_loop(0, HG_CHUNK // SUBLANES, body, 0)
        finish(ts, rows_out)

    rows_a, rows_b = slice(0, tt), slice(tt, 2 * tt)
    g_min_a = recur(tiles[0], rows_a, projection_pieces(x_ref[0, rows_b, :].astype(BF16), tiles[1]))
    g_min_b = recur(tiles[1], rows_b, [functools.partial(finish, tiles[0], rows_a)]
                    + projection_pieces(xn_ref[0].astype(BF16), tiles[0]))
    finish(tiles[1], rows_b)

    @pl.when(jnp.min(jnp.minimum(g_min_a, g_min_b)) <= -HG_SAFE_DECAY)
    def _():
        redo_exact(tiles[0], rows_a)
        redo_exact(tiles[1], rows_b)

    @pl.when(t == nt - 1)
    def _():
        for h in range(ML_HEADS):
            caug_out_ref[0, h] = caug_scr[h]
            s_out_ref[0, h] = st_scr[h].T
        m_out_ref[0] = m_scr[...]
        conv_out_ref[0] = xp_scr[CONV_PAD - 3:CONV_PAD, :]


def _const_spec(shape):
    nd = len(shape)
    return pl.BlockSpec(shape, lambda *_: (0,) * nd, pipeline_mode=pl.Buffered(1))


def _skip_alias_refs(body, n_in, n_alias, *refs):
    return body(*refs[:n_in], *refs[n_in + n_alias:])


def _alias_args(prev, n_in):
    if prev is None:
        return [], (), {}
    specs = [pl.BlockSpec(memory_space=pl.ANY)] * len(prev)
    return specs, tuple(prev), {n_in + k: 1 + k for k in range(len(prev))}


def _prompt_mixer(x, p, tt, layer, prev):
    bsz, seq, _ = x.shape
    consts = (p["w_main"], p["w_gate"], p["b_main"], p["b_gate"], p["conv_w"], p["conv_b"], p["ml_g"],
              p["hg_g"], p["log_lb"], p["log1m_lb"], p["one_m_lb"], p["w_out"], p["ln1_g"], p["ln1_b"])
    out_shape = (
        jax.ShapeDtypeStruct((bsz, seq, D_MODEL), F32),
        jax.ShapeDtypeStruct((DEPTH, bsz, CONV_WIDTH - 1, 2 * ML_WIDTH), F32),
        jax.ShapeDtypeStruct((DEPTH, bsz, ML_HEADS, ML_DK, 2 * ML_DK), F32),
        jax.ShapeDtypeStruct((DEPTH, bsz, SUBLANES, LANES), F32),
        jax.ShapeDtypeStruct((DEPTH, bsz, HG_HEADS, HG_DK, HG_DK), F32),
    )
    out_specs = (
        pl.BlockSpec((1, 2 * tt, D_MODEL), lambda b, t: (b, t, 0)),
        pl.BlockSpec((None, 1, CONV_WIDTH - 1, 2 * ML_WIDTH), lambda b, t: (layer, b, 0, 0)),
        pl.BlockSpec((None, 1, ML_HEADS, ML_DK, 2 * ML_DK), lambda b, t: (layer, b, 0, 0, 0)),
        pl.BlockSpec((None, 1, SUBLANES, LANES), lambda b, t: (layer, b, 0, 0)),
        pl.BlockSpec((None, 1, HG_HEADS, HG_DK, HG_DK), lambda b, t: (layer, b, 0, 0, 0)),
    )
    n_in = 2 + len(consts)
    alias_specs, alias_ops, alias_map = _alias_args(prev, n_in)
    tile_scratch = [
        pltpu.VMEM((tt, MAIN_COLS), F32),
        pltpu.VMEM((tt, GATE_COLS), F32),
        pltpu.VMEM((tt, HG_WIDTH), F32),
        pltpu.VMEM((tt, HG_WIDTH), F32),
        pltpu.VMEM((tt, HG_WIDTH), F32),
        pltpu.VMEM((tt, HG_WIDTH), F32),
        pltpu.VMEM((tt, HG_WIDTH), F32),
        pltpu.VMEM((tt, HG_WIDTH), F32),
        pltpu.VMEM((tt, HG_WIDTH), F32),
        pltpu.VMEM((tt, 2 * ML_WIDTH), BF16),
    ]
    assert len(tile_scratch) == _TileScratch.N_REFS
    scratch = [
        pltpu.VMEM((CONV_PAD + tt, 2 * ML_WIDTH), F32),
        pltpu.VMEM((ML_HEADS, ML_DK, 2 * ML_DK), F32),
        pltpu.VMEM((SUBLANES, LANES), F32),
        pltpu.VMEM((HG_HEADS, HG_DK, HG_DK), F32),
    ] + tile_scratch + tile_scratch
    n_pairs = seq // (2 * tt)
    last_pair = bsz * n_pairs - 1

    def next_first_tile(b, t):
        nxt = jnp.minimum(b * n_pairs + t + 1, last_pair)
        return (nxt // n_pairs, 2 * (nxt % n_pairs), 0)

    return pl.pallas_call(
        functools.partial(_skip_alias_refs, functools.partial(_prompt_mixer_pair_kernel, tt=tt), n_in,
                          len(alias_ops)),
        grid=(bsz, n_pairs),
        in_specs=([pl.BlockSpec((1, 2 * tt, D_MODEL), lambda b, t: (b, t, 0)),
                   pl.BlockSpec((1, tt, D_MODEL), next_first_tile)]
                  + [_const_spec(c.shape) for c in consts] + alias_specs),
        out_specs=out_specs,
        out_shape=out_shape,
        scratch_shapes=scratch,
        input_output_aliases=alias_map,
        compiler_params=pltpu.CompilerParams(dimension_semantics=("arbitrary", "arbitrary"),
                                             vmem_limit_bytes=VMEM_LIMIT),
        name="prompt_mixer",
    )(x, x, *consts, *alias_ops)


DEC_BLOCK = SUBLANES


def _decode_mixer_kernel(x_ref, conv_ref, c_ref, n_ref, m_ref, s_ref,
                         w_main_ref, w_gate_ref, b_main_ref, b_gate_ref, conv_w_ref, conv_b_ref,
                         ml_g_ref, hg_g_ref, log_lb_ref, log1m_lb_ref, one_m_lb_ref, w_out_ref,
                         ln_g_ref, ln_b_ref,
                         x1_ref, conv_out_ref, c_out_ref, n_out_ref, m_out_ref, s_out_ref,
                         q_scr, k_scr, v_scr, og_scr, li_scr, lf_scr, hq_scr, hk_scr, hf_scr, hi_scr, hgate_scr,
                         qt_scr, kt_scr, hqt_scr, hkt_scr, hft_scr, hml_scr, ho_scr, *, nb):
    i = pl.program_id(0)
    nsteps = pl.num_programs(0)

    @pl.when(i == 0)
    def _():
        xb = x_ref[...].astype(BF16)

        def proj(lo, hi):
            return _dot(xb, w_main_ref[:, lo:hi]) + b_main_ref[:, lo:hi]

        qk_raw = proj(0, 2 * ML_WIDTH)
        w = 2 * ML_WIDTH
        qk = conv_b_ref[...] + qk_raw * conv_w_ref[CONV_WIDTH - 1:CONV_WIDTH, :]
        for j in range(CONV_WIDTH - 1):
            qk = qk + conv_ref[:, j * w:(j + 1) * w] * conv_w_ref[j:j + 1, :]
        qk = _silu(qk)
        conv_out_ref[:, 0:2 * w] = conv_ref[:, w:3 * w]
        conv_out_ref[:, 2 * w:3 * w] = qk_raw
        q_scr[...] = qk[:, :ML_WIDTH]
        k_scr[...] = qk[:, ML_WIDTH:] * (ML_DK ** -0.5)
        v_scr[...] = proj(2 * ML_WIDTH, 3 * ML_WIDTH)
        og_scr[...] = _sigmoid(proj(3 * ML_WIDTH, 4 * ML_WIDTH))
        zg = _dot(xb, w_gate_ref[...]) + b_gate_ref[...]
        li_scr[...] = zg[:, 0:LANES]
        lf_scr[...] = _log_sigmoid(zg[:, LANES:2 * LANES])
        base = 4 * ML_WIDTH
        hq_scr[...] = _silu(proj(base, base + HG_WIDTH))
        lf_h, k_h = _hg_gates(proj(base + HG_WIDTH, base + 2 * HG_WIDTH),
                              log_lb_ref[...], log1m_lb_ref[...], one_m_lb_ref[...])
        hk_scr[...] = k_h
        hf_scr[...] = jnp.exp(lf_h)
        hi_scr[...] = proj(base + 2 * HG_WIDTH, base + 3 * HG_WIDTH)
        hgate_scr[...] = _silu(proj(base + 3 * HG_WIDTH, base + 4 * HG_WIDTH))
        def split_t(blk):
            xt = blk.T
            hi = xt.astype(BF16)
            return jnp.concatenate([hi, (xt - hi.astype(F32)).astype(BF16)], axis=1)

        for h in range(ML_HEADS):
            hs = slice(h * ML_DK, (h + 1) * ML_DK)
            qt_scr[h] = split_t(q_scr[:, hs])
            kt_scr[h] = split_t(k_scr[:, hs])
            hqt_scr[h] = split_t(hq_scr[:, hs])
            hkt_scr[h] = split_t(hk_scr[:, hs])
            hft_scr[h] = split_t(hf_scr[:, hs])

    b0 = pl.multiple_of(i * DEC_BLOCK, DEC_BLOCK)
    rows = pl.ds(b0, DEC_BLOCK)
    pick_seq = lax.broadcasted_iota(jnp.int32, (2 * nb, DEC_BLOCK * LANES), 0) & (nb - 1)
    pick_slab = lax.broadcasted_iota(jnp.int32, (2 * nb, DEC_BLOCK * LANES), 1) >> 7
    spread = jnp.where(pick_seq == b0 + pick_slab, 1.0, 0.0).astype(BF16)
    li = li_scr[rows, :]
    lf = lf_scr[rows, :]
    m_prev = m_ref[...]
    q_blk = q_scr[rows, :]
    k_blk = k_scr[rows, :]
    v_blk = v_scr[rows, :]
    hi_blk = hi_scr[rows, :]
    n_blk = n_ref[...]
    m_new_cols = []
    for h in range(ML_HEADS):
        hs = slice(h * ML_DK, (h + 1) * ML_DK)
        li_h = li[:, h:h + 1]
        lf_h = lf[:, h:h + 1]
        mp_h = m_prev[:, h:h + 1]
        m_t = jnp.maximum(lf_h + mp_h, li_h)
        dmat = jnp.exp(li_h - m_t)
        dec = jnp.exp(lf_h + mp_h - m_t)
        n_h = n_blk[:, hs]
        s = jnp.sum(q_blk[:, hs] * k_blk[:, hs], axis=1, keepdims=True) * dmat
        qn = s + dec * jnp.sum(q_blk[:, hs] * n_h, axis=1, keepdims=True)
        denom = jnp.maximum(jnp.abs(qn), jnp.exp(-m_t))
        n_out_ref[:, hs] = dec * n_h + dmat * k_blk[:, hs]
        m_new_cols.append(m_t)
        qt = _dot(qt_scr[h], spread)
        kt = _dot(kt_scr[h], spread)
        hqt = _dot(hqt_scr[h], spread)
        hkt = _dot(hkt_scr[h], spread)
        hft = _dot(hft_scr[h], spread)
        qc_rows, ho_rows = [], []
        for j in range(DEC_BLOCK):
            jr = slice(j, j + 1)
            js = slice(j * LANES, (j + 1) * LANES)
            cmat = c_ref[j, h]
            qc_rows.append(jnp.sum(qt[:, js] * cmat, axis=0, keepdims=True))
            c_out_ref[j, h] = dec[jr, :] * cmat + (dmat[jr, :] * kt[:, js]) * v_blk[jr, hs]
            s_new = hft[:, js] * s_ref[j, h] + hkt[:, js] * hi_blk[jr, hs]
            s_out_ref[j, h] = s_new
            ho_rows.append(jnp.sum(hqt[:, js] * s_new, axis=0, keepdims=True))
        num = s * v_blk[:, hs] + dec * jnp.concatenate(qc_rows, axis=0)
        hml_scr[rows, hs] = num / denom
        ho_scr[rows, hs] = jnp.concatenate(ho_rows, axis=0)
    m_out_ref[...] = jnp.concatenate(m_new_cols, axis=1)

    @pl.when(i == nsteps - 1)
    def _():
        for h in range(ML_HEADS):
            hs = slice(h * ML_DK, (h + 1) * ML_DK)
            hm = _head_rms(og_scr[:, hs] * hml_scr[:, hs], ml_g_ref[:, hs])
            hml_scr[:, hs] = hm
            ho = _head_rms(ho_scr[:, hs], hg_g_ref[:, hs]) * hgate_scr[:, hs]
            ho_scr[:, hs] = ho
        mix = (_dot(hml_scr[...].astype(BF16), w_out_ref[0:ML_WIDTH, :])
               + _dot(ho_scr[...].astype(BF16), w_out_ref[ML_WIDTH:2 * ML_WIDTH, :]))
        x1_ref[...] = _layer_norm_rows(DEEPNORM_ALPHA * x_ref[...] + mix, ln_g_ref[...], ln_b_ref[...])


def _decode_mixer(x, conv, c_state, n_state, m_state, s_state, p, layer, prev):
    nb = x.shape[0]
    assert nb == LANES, "decode batch must fill one lane tile"
    consts = (p["w_main"], p["w_gate"], p["b_main"], p["b_gate"], p["conv_w"], p["conv_b"], p["ml_g"],
              p["hg_g"], p["log_lb"], p["log1m_lb"], p["one_m_lb"], p["w_out"], p["ln1_g"], p["ln1_b"])
    conv_cols = (CONV_WIDTH - 1) * 2 * ML_WIDTH
    blk4 = lambda i: (layer, i, 0, 0, 0)
    blk2 = lambda i: (layer, i, 0)
    conv_spec = pl.BlockSpec((None, nb, conv_cols), lambda i: (layer, 0, 0))
    state_specs = [
        pl.BlockSpec((None, DEC_BLOCK, ML_HEADS, ML_DK, ML_DK), blk4),
        pl.BlockSpec((None, DEC_BLOCK, ML_WIDTH), blk2),
        pl.BlockSpec((None, DEC_BLOCK, ML_HEADS), blk2),
        pl.BlockSpec((None, DEC_BLOCK, HG_HEADS, HG_DK, HG_DK), blk4),
    ]
    full2 = lambda shape: pl.BlockSpec(shape, lambda i: (0, 0))
    out_shape = (
        jax.ShapeDtypeStruct((nb, D_MODEL), F32),
        jax.ShapeDtypeStruct(conv.shape, F32),
        jax.ShapeDtypeStruct(c_state.shape, F32),
        jax.ShapeDtypeStruct(n_state.shape, F32),
        jax.ShapeDtypeStruct(m_state.shape, F32),
        jax.ShapeDtypeStruct(s_state.shape, F32),
    )
    n_in = 6 + len(consts)
    alias_specs, alias_ops, alias_map = _alias_args(prev, n_in)
    rows_f32 = lambda w: pltpu.VMEM((nb, w), F32)
    scratch = ([rows_f32(ML_WIDTH)] * 4 + [rows_f32(LANES)] * 2 + [rows_f32(HG_WIDTH)] * 5
               + [pltpu.VMEM((ML_HEADS, ML_DK, 2 * nb), BF16)] * 5 + [rows_f32(ML_WIDTH), rows_f32(HG_WIDTH)])
    return pl.pallas_call(
        functools.partial(_skip_alias_refs, functools.partial(_decode_mixer_kernel, nb=nb), n_in, len(alias_ops)),
        grid=(nb // DEC_BLOCK,),
        in_specs=([full2((nb, D_MODEL)), conv_spec] + state_specs + [_const_spec(c.shape) for c in consts]
                  + alias_specs),
        out_specs=(full2((nb, D_MODEL)), conv_spec) + tuple(state_specs),
        out_shape=out_shape,
        scratch_shapes=scratch,
        input_output_aliases=alias_map,
        compiler_params=pltpu.CompilerParams(dimension_semantics=("arbitrary",), vmem_limit_bytes=VMEM_LIMIT),
        name="decode_mixer",
    )(x, conv, c_state, n_state, m_state, s_state, *consts, *alias_ops)


def _moe_kernel(x_ref, w_r_ref, b_r_ref, w1_ref, w3_ref, w2_ref, ln_g_ref, ln_b_ref, y_ref, hdn_scr):
    x = x_ref[...]
    xb = x.astype(BF16)
    logits = _dot(xb, w_r_ref[...]) + b_r_ref[...]
    lg = logits[:, 0:N_GROUPS]
    le = logits[:, N_GROUPS:N_GROUPS + N_EXPERTS]
    gidx = lax.broadcasted_iota(jnp.int32, lg.shape, 1)
    gmax = jnp.max(lg, axis=1, keepdims=True)
    g = jnp.min(jnp.where(lg == gmax, gidx, N_GROUPS), axis=1, keepdims=True)
    pg = 1.0 / jnp.sum(jnp.exp(lg - gmax), axis=1, keepdims=True)
    eidx = lax.broadcasted_iota(jnp.int32, le.shape, 1)
    in_g = (eidx >> 2) == g
    lm = jnp.where(in_g, le, -jnp.inf)
    v1 = jnp.max(lm, axis=1, keepdims=True)
    i1 = jnp.min(jnp.where(lm == v1, eidx, N_EXPERTS), axis=1, keepdims=True)
    lm2 = jnp.where(eidx == i1, -jnp.inf, lm)
    v2 = jnp.max(lm2, axis=1, keepdims=True)
    i2 = jnp.min(jnp.where(lm2 == v2, eidx, N_EXPERTS), axis=1, keepdims=True)
    e21 = jnp.exp(v2 - v1)
    wt1 = pg / (1.0 + e21)
    wt2 = wt1 * e21
    gate = jnp.where(eidx == i1, wt1, 0.0) + jnp.where(eidx == i2, wt2, 0.0)
    for e in range(N_EXPERTS):
        a = _dot(xb, w1_ref[e])
        u = _dot(xb, w3_ref[e])
        hdn = _silu(a) * u * gate[:, e:e + 1]
        hdn_scr[:, e * EXPERT_FF:(e + 1) * EXPERT_FF] = hdn.astype(BF16)
    moe = _dot(hdn_scr[...], w2_ref[...])
    y_ref[...] = _layer_norm_rows(DEEPNORM_ALPHA * x + moe, ln_g_ref[...], ln_b_ref[...])


def _moe(x, p, tm):
    n = x.shape[0]
    consts = (p["w_router"], p["b_router"], p["w_gate_e"], p["w_up_e"], p["w_down_e"], p["ln2_g"], p["ln2_b"])
    return pl.pallas_call(
        _moe_kernel,
        grid=(n // tm,),
        in_specs=[pl.BlockSpec((tm, D_MODEL), lambda i: (i, 0))] + [_const_spec(c.shape) for c in consts],
        out_specs=pl.BlockSpec((tm, D_MODEL), lambda i: (i, 0)),
        out_shape=jax.ShapeDtypeStruct((n, D_MODEL), F32),
        scratch_shapes=[pltpu.VMEM((tm, N_EXPERTS * EXPERT_FF), BF16)],
        compiler_params=pltpu.CompilerParams(dimension_semantics=("arbitrary",), vmem_limit_bytes=VMEM_LIMIT),
        name="moe",
    )(x, *consts)


MOE_BLOCK = 128
GROUP_FF = EXPERTS_PER_GROUP * EXPERT_FF


def _moe_grouped_kernel(x_ref, w_rt_ref, b_rt_ref, w1_ref, w3_ref, w2_ref, ln_g_ref, ln_b_ref, y_ref,
                        xb_scr, gt_scr, pos_scr, acc_scr, *, tm):
    x = x_ref[...]
    xb = x.astype(BF16)
    xb_scr[...] = xb
    logits = _dot_nt(w_rt_ref[...], xb) + b_rt_ref[...]
    ridx8 = lax.broadcasted_iota(jnp.int32, (SUBLANES, tm), 0)
    lg = jnp.where(ridx8 < N_GROUPS, logits[N_EXPERTS:N_EXPERTS + SUBLANES, :], -jnp.inf)
    gmax = jnp.max(lg, axis=0, keepdims=True)
    g = jnp.min(jnp.where(lg == gmax, ridx8, SUBLANES), axis=0, keepdims=True)
    pg = 1.0 / jnp.sum(jnp.exp(lg - gmax), axis=0, keepdims=True)
    le = logits[0:N_EXPERTS, :]
    eidx = lax.broadcasted_iota(jnp.int32, (N_EXPERTS, tm), 0)
    lm = jnp.where((eidx >> 2) == g, le, -jnp.inf)
    v1 = jnp.max(lm, axis=0, keepdims=True)
    i1 = jnp.min(jnp.where(lm == v1, eidx, N_EXPERTS), axis=0, keepdims=True)
    lm2 = jnp.where(eidx == i1, -jnp.inf, lm)
    v2 = jnp.max(lm2, axis=0, keepdims=True)
    i2 = jnp.min(jnp.where(lm2 == v2, eidx, N_EXPERTS), axis=0, keepdims=True)
    e21 = jnp.exp(v2 - v1)
    wt1 = pg / (1.0 + e21)
    wt2 = wt1 * e21
    gate = jnp.where(eidx == i1, wt1, 0.0) + jnp.where(eidx == i2, wt2, 0.0)
    gate4 = gate[0:4, :] + gate[4:8, :] + gate[8:12, :] + gate[12:16, :]
    g_hi = gate4.astype(BF16)
    g_lo = (gate4 - g_hi.astype(F32)).astype(BF16)
    gt_scr[...] = jnp.zeros_like(gt_scr)
    gt_scr[0:4, :] = g_hi
    gt_scr[SUBLANES:SUBLANES + 4, :] = g_lo

    onehot = jnp.where(ridx8 == g, 1.0, 0.0)
    tr = lax.broadcasted_iota(jnp.int32, (tm, tm), 0)
    tc = lax.broadcasted_iota(jnp.int32, (tm, tm), 1)
    before = jnp.where(tr < tc, 1.0, 0.0).astype(BF16)
    rank = _dot(onehot.astype(BF16), before)
    counts = jnp.sum(onehot, axis=1, keepdims=True).astype(jnp.int32)
    nblk_g = (counts + (MOE_BLOCK - 1)) >> (MOE_BLOCK.bit_length() - 1)
    s1 = nblk_g[0:1, :]
    s2 = s1 + nblk_g[1:2, :]
    s3 = s2 + nblk_g[2:3, :]
    n_blocks = s3 + nblk_g[3:4, :]
    cidx = lax.broadcasted_iota(jnp.int32, (SUBLANES, 1), 0)
    start_blk = jnp.where(cidx == 1, s1, 0) + jnp.where(cidx == 2, s2, 0) + jnp.where(cidx == 3, s3, 0)
    pos = jnp.sum(onehot * ((start_blk * MOE_BLOCK).astype(F32) + rank), axis=0, keepdims=True)
    pos_scr[...] = jnp.broadcast_to(pos, (SUBLANES, tm))
    acc_scr[...] = jnp.zeros_like(acc_scr)
    b1 = s1[0, 0]
    b2 = s2[0, 0]
    b3 = s3[0, 0]

    def block(b, carry):
        grp = jnp.where(b >= b1, 1, 0) + jnp.where(b >= b2, 1, 0) + jnp.where(b >= b3, 1, 0)
        slot = lax.broadcasted_iota(jnp.int32, (MOE_BLOCK, 1), 0) + b * MOE_BLOCK
        sel = jnp.where(pos_scr[0:1, :] == slot.astype(F32), 1.0, 0.0).astype(BF16)
        xs = _dot(sel, xb_scr[...]).astype(BF16)
        gs = _dot_nt(sel, gt_scr[...])
        gsel = gs[:, 0:4] + gs[:, SUBLANES:SUBLANES + 4]
        parts = []
        for j in range(EXPERTS_PER_GROUP):
            e = grp * EXPERTS_PER_GROUP + j
            a = _dot(xs, w1_ref[e])
            u = _dot(xs, w3_ref[e])
            parts.append((_silu(a) * u * gsel[:, j:j + 1]).astype(BF16))
        hdn = jnp.concatenate(parts, axis=1)
        ys = _dot(hdn, w2_ref[pl.ds(pl.multiple_of(grp * GROUP_FF, GROUP_FF), GROUP_FF), :])
        y_hi = ys.astype(BF16)
        y_lo = (ys - y_hi.astype(F32)).astype(BF16)
        acc_scr[...] += _dot_tn(jnp.concatenate([sel, sel], axis=0), jnp.concatenate([y_hi, y_lo], axis=0))
        return carry

    lax.fori_loop(0, n_blocks[0, 0], block, 0)
    y_ref[...] = _layer_norm_rows(DEEPNORM_ALPHA * x + acc_scr[...], ln_g_ref[...], ln_b_ref[...])


def _moe_grouped(x, p, tm):
    n = x.shape[0]
    consts = (p["w_router_t"], p["b_router_t"], p["w_gate_e"], p["w_up_e"], p["w_down_e"], p["ln2_g"], p["ln2_b"])
    return pl.pallas_call(
        functools.partial(_moe_grouped_kernel, tm=tm),
        grid=(n // tm,),
        in_specs=[pl.BlockSpec((tm, D_MODEL), lambda i: (i, 0))] + [_const_spec(c.shape) for c in consts],
        out_specs=pl.BlockSpec((tm, D_MODEL), lambda i: (i, 0)),
        out_shape=jax.ShapeDtypeStruct((n, D_MODEL), F32),
        scratch_shapes=[pltpu.VMEM((tm, D_MODEL), BF16),
                        pltpu.VMEM((2 * SUBLANES, tm), BF16),
                        pltpu.VMEM((SUBLANES, tm), F32),
                        pltpu.VMEM((tm, D_MODEL), F32)],
        compiler_params=pltpu.CompilerParams(dimension_semantics=("arbitrary",), vmem_limit_bytes=VMEM_LIMIT),
        name="moe_grouped",
    )(x, *consts)


def _layer_params(l, lbs, w_in, b_in, conv_w, conv_b, ml_norm_g, hg_norm_g, w_out, ln1_g, ln1_b,
                  w_router_group, b_router_group, w_router_expert, b_router_expert,
                  w_gate, w_up, w_down, ln2_g, ln2_b):
    g0 = 4 * ML_WIDTH
    w = w_in[l]
    b = b_in[l]
    w_main = jnp.concatenate([w[:, :g0], w[:, ML_COLS:]], axis=1).astype(BF16)
    b_main = jnp.concatenate([b[:g0], b[ML_COLS:]])[None, :]
    zw = jnp.zeros((D_MODEL, LANES - ML_HEADS), F32)
    zb = jnp.zeros((LANES - ML_HEADS,), F32)
    w_gate_cols = jnp.concatenate([w[:, g0:g0 + ML_HEADS], zw, w[:, g0 + ML_HEADS:ML_COLS], zw], axis=1).astype(BF16)
    b_gate_cols = jnp.concatenate([b[g0:g0 + ML_HEADS], zb, b[g0 + ML_HEADS:ML_COLS], zb])[None, :]
    pad_r = LANES - N_GROUPS - N_EXPERTS
    w_router = jnp.concatenate([w_router_group[l], w_router_expert[l], jnp.zeros((D_MODEL, pad_r), F32)],
                               axis=1).astype(BF16)
    b_router = jnp.concatenate([b_router_group[l], b_router_expert[l], jnp.zeros((pad_r,), F32)])[None, :]
    w_router_t = jnp.concatenate([w_router_expert[l], w_router_group[l], jnp.zeros((D_MODEL, pad_r), F32)],
                                 axis=1).T.astype(BF16)
    b_router_t = jnp.concatenate([b_router_expert[l], b_router_group[l], jnp.zeros((pad_r,), F32)])[:, None]
    lb = lbs[l]
    return {
        "w_router_t": w_router_t, "b_router_t": b_router_t,
        "w_main": w_main, "b_main": b_main, "w_gate": w_gate_cols, "b_gate": b_gate_cols,
        "conv_w": conv_w[l], "conv_b": conv_b[l][None, :], "ml_g": ml_norm_g[l][None, :],
        "hg_g": hg_norm_g[l][None, :], "log_lb": jnp.log(lb)[None, :], "log1m_lb": jnp.log1p(-lb)[None, :],
        "one_m_lb": (1.0 - lb)[None, :], "w_out": w_out[l].astype(BF16),
        "ln1_g": ln1_g[l][None, :], "ln1_b": ln1_b[l][None, :],
        "w_router": w_router, "b_router": b_router,
        "w_gate_e": w_gate[l].astype(BF16), "w_up_e": w_up[l].astype(BF16),
        "w_down_e": w_down[l].astype(BF16).reshape(N_EXPERTS * EXPERT_FF, D_MODEL),
        "ln2_g": ln2_g[l][None, :], "ln2_b": ln2_b[l][None, :],
    }


def _pick_tile(n, pref):
    t = pref
    while n % t:
        t //= 2
    return t


def kernel(x_prompt, x_sample, state_conv, state_ml_C, state_ml_n, state_ml_m, state_hg_S, w_in, b_in, conv_w, conv_b, ml_norm_g, hg_lower_bounds, hg_norm_g, w_out, ln1_g, ln1_b, w_router_group, b_router_group, w_router_expert, b_router_expert, w_gate, w_up, w_down, ln2_g, ln2_b):
    sm = jax.nn.softmax(hg_lower_bounds.astype(F32), axis=0)
    lbs = jnp.concatenate([jnp.zeros_like(sm[:1]), jnp.cumsum(sm[1:], axis=0)], axis=0)
    bsz, seq, _ = x_prompt.shape
    nb = x_sample.shape[0]
    tt = _pick_tile(seq, 256)
    y_p = x_prompt
    y_s = x_sample.reshape(nb, D_MODEL)
    n_tok = bsz * seq
    conv_in = state_conv.reshape(DEPTH, nb, -1)
    n_in = state_ml_n.reshape(DEPTH, nb, ML_WIDTH)
    st_p = st_s = None
    for l in range(DEPTH):
        p = _layer_params(l, lbs, w_in, b_in, conv_w, conv_b, ml_norm_g, hg_norm_g, w_out, ln1_g, ln1_b,
                          w_router_group, b_router_group, w_router_expert, b_router_expert,
                          w_gate, w_up, w_down, ln2_g, ln2_b)
        x1, *st_p = _prompt_mixer(y_p, p, tt, l, st_p)
        y_p = _moe_grouped(x1.reshape(n_tok, D_MODEL), p, _pick_tile(n_tok, 512)).reshape(bsz, seq, D_MODEL)
        x1s, *st_s = _decode_mixer(y_s, conv_in, state_ml_C, n_in, state_ml_m, state_hg_S, p, l, st_s)
        y_s = _moe(x1s, p, _pick_tile(nb, 256))
    conv_p, caug, m_p, s_p = st_p
    conv_s, c_s, n_s, m_s, s_s = st_s
    return (y_p, y_s.reshape(nb, 1, D_MODEL),
            conv_p, caug[..., :ML_DK], caug[..., ML_DK], m_p[:, :, :ML_HEADS, 0], s_p,
            conv_s.reshape(DEPTH, nb, CONV_WIDTH - 1, 2 * ML_WIDTH), c_s,
            n_s.reshape(DEPTH, nb, ML_HEADS, ML_DK), m_s, s_s)
```

```python
import functools

import jax
import jax.numpy as jnp
from jax import lax
from jax.experimental import pallas as pl
from jax.experimental.pallas import tpu as pltpu

F32 = jnp.float32
BF16 = jnp.bfloat16

D_MODEL = 1024
DEPTH = 2
ML_HEADS = 4
ML_WIDTH = 512
ML_DK = 128
ML_CHUNK = 128
CONV_WIDTH = 4
HG_HEADS = 4
HG_WIDTH = 512
HG_DK = 128
HG_CHUNK = 64
ML_COLS = 4 * ML_WIDTH + 2 * ML_HEADS
N_GROUPS = 4
EXPERTS_PER_GROUP = 4
N_EXPERTS = 16
EXPERT_FF = 256
DEEPNORM_ALPHA = (2 * DEPTH) ** 0.25
LN_EPS = 1e-5
RMS_EPS = 1e-6

LANES = 128
SUBLANES = 8
MAIN_COLS = 4 * ML_WIDTH + 4 * HG_WIDTH
GATE_COLS = LANES
HALF_COLS = MAIN_COLS // 2


def _main_weight(w_a_ref, w_b_ref, lo, hi):
    if hi <= HALF_COLS:
        return w_a_ref[:, lo:hi]
    return w_b_ref[:, lo - HALF_COLS:hi - HALF_COLS]


def _forget_slab(zg):
    return pltpu.roll(zg, LANES - ML_HEADS, 1)
CONV_PAD = SUBLANES
HG_SAFE_DECAY = 150.0
VMEM_LIMIT = 56 * 1024 * 1024


def _dot(a, b):
    return jnp.dot(a, b, preferred_element_type=F32)


def _dot_nt(a, b):
    return lax.dot_general(a, b, (((1,), (1,)), ((), ())), preferred_element_type=F32)


def _dot_tn(a, b):
    return lax.dot_general(a, b, (((0,), (0,)), ((), ())), preferred_element_type=F32)


def _sigmoid(x):
    return 1.0 / (1.0 + jnp.exp(-x))


def _silu(x):
    return x * _sigmoid(x)


def _log_sigmoid(x):
    return jnp.minimum(x, 0.0) - jnp.log(1.0 + jnp.exp(-jnp.abs(x)))


def _cumsum_rows(tri_bf, x):
    hi = x.astype(BF16)
    lo = (x - hi.astype(F32)).astype(BF16)
    return _dot(tri_bf, hi) + _dot(tri_bf, lo)


def _layer_norm_rows(x, g, b):
    mu = jnp.mean(x, axis=-1, keepdims=True)
    xc = x - mu
    var = jnp.mean(xc * xc, axis=-1, keepdims=True)
    return xc * lax.rsqrt(var + LN_EPS) * g + b


def _head_rms(h, g):
    return h * lax.rsqrt(jnp.mean(h * h, axis=-1, keepdims=True) + RMS_EPS) * g


def _hg_gates(hf, log_lb, log1m_lb, one_m_lb):
    e = jnp.exp(-jnp.abs(hf))
    one_p_e = 1.0 + e
    log_sig = jnp.minimum(hf, 0.0) - jnp.log(one_p_e)
    b = log1m_lb + log_sig
    hi = jnp.maximum(log_lb, b)
    lo = jnp.minimum(log_lb, b)
    lf = hi + jnp.log(1.0 + jnp.exp(lo - hi))
    k = one_m_lb * jnp.where(hf >= 0.0, e, 1.0) / one_p_e
    return lf, k


class _TileScratch:
    N_REFS = 10

    def __init__(self, z, zg, hq, hk, hi, hgc, ho, hinter, hgate, mixed):
        self.z, self.zg, self.hq, self.hk, self.hi, self.hgc = z, zg, hq, hk, hi, hgc
        self.ho, self.hinter, self.hgate, self.mixed = ho, hinter, hgate, mixed


def _prompt_mixer_pair_kernel(x_ref, xn_ref, w_a_ref, w_b_ref, w_gate_ref, b_main_ref, b_gate_ref, conv_w_ref, conv_b_ref,
                              ml_g_ref, hg_g_ref, log_lb_ref, log1m_lb_ref, one_m_lb_ref, w_out_ref,
                              ln_g_ref, ln_b_ref,
                              x1_ref, conv_out_ref, caug_out_ref, m_out_ref, s_out_ref,
                              xp_scr, caug_scr, m_scr, st_scr, *tile_scr, tt):
    t = pl.program_id(1)
    nt = pl.num_programs(1)
    n_refs = _TileScratch.N_REFS
    tiles = (_TileScratch(*tile_scr[0:n_refs]), _TileScratch(*tile_scr[n_refs:2 * n_refs]))

    @pl.when(t == 0)
    def _():
        xp_scr[0:CONV_PAD, :] = jnp.zeros((CONV_PAD, 2 * ML_WIDTH), F32)
        caug_scr[...] = jnp.zeros_like(caug_scr)
        m_scr[...] = jnp.zeros_like(m_scr)
        st_scr[...] = jnp.zeros_like(st_scr)

    piece_cols = 2 * LANES
    sections = [(lo, lo + piece_cols) for lo in range(0, MAIN_COLS, piece_cols)]

    def projection_pieces(src_bf, dst):
        def main(lo, hi):
            dst.z[:, lo:hi] = _dot(src_bf, _main_weight(w_a_ref, w_b_ref, lo, hi))

        def gates():
            dst.zg[...] = _dot(src_bf, w_gate_ref[...])

        return [functools.partial(main, lo, hi) for lo, hi in sections] + [gates]

    @pl.when((pl.program_id(0) == 0) & (t == 0))
    def _():
        for piece in projection_pieces(x_ref[0, 0:tt, :].astype(BF16), tiles[0]):
            piece()

    row = lax.broadcasted_iota(jnp.int32, (ML_CHUNK, ML_CHUNK), 0)
    col = lax.broadcasted_iota(jnp.int32, (ML_CHUNK, ML_CHUNK), 1)
    tril = row >= col
    tri_bf = jnp.where(tril, 1.0, 0.0).astype(BF16)
    ones_col = jnp.where(lax.broadcasted_iota(jnp.int32, (ML_CHUNK, LANES), 1) == 0, 1.0, 0.0).astype(BF16)
    tril64 = tril[:HG_CHUNK, :HG_CHUNK]
    row64 = row[:HG_CHUNK, :1]
    rt = lax.broadcasted_iota(jnp.int32, (tt, tt), 0)
    ct = lax.broadcasted_iota(jnp.int32, (tt, tt), 1)
    blk_tri = jnp.where((rt >= ct) & ((rt >> 6) == (ct >> 6)), 1.0, 0.0).astype(BF16)
    base = 4 * ML_WIDTH

    def finish(ts, rows_out):
        for h in range(HG_HEADS):
            hs = slice(h * HG_DK, (h + 1) * HG_DK)
            o = _head_rms(ts.ho[:, hs], hg_g_ref[:, hs]) * ts.hgate[:, hs]
            ts.mixed[:, ML_WIDTH + h * HG_DK:ML_WIDTH + (h + 1) * HG_DK] = o.astype(BF16)
        mix = _dot(ts.mixed[...], w_out_ref[...])
        x1_ref[0, rows_out, :] = _layer_norm_rows(DEEPNORM_ALPHA * x_ref[0, rows_out, :] + mix,
                                                  ln_g_ref[...], ln_b_ref[...])

    def recur(ts, rows_out, fillers):
        def fill():
            if fillers:
                fillers.pop(0)()

        def proj(lo, hi):
            return ts.z[:, lo:hi] + b_main_ref[:, lo:hi]

        xp_scr[CONV_PAD:CONV_PAD + tt, :] = proj(0, 2 * ML_WIDTH)
        fill()
        qk = conv_b_ref[...] + xp_scr[CONV_PAD - 3:CONV_PAD - 3 + tt, :] * conv_w_ref[0:1, :]
        for j in range(1, CONV_WIDTH):
            qk = qk + xp_scr[CONV_PAD - 3 + j:CONV_PAD - 3 + j + tt, :] * conv_w_ref[j:j + 1, :]
        qk = _silu(qk)
        xp_scr[CONV_PAD - 3:CONV_PAD, :] = xp_scr[CONV_PAD + tt - 3:CONV_PAD + tt, :]
        q_all = qk[:, :ML_WIDTH].astype(BF16)
        k_all = qk[:, ML_WIDTH:] * (ML_DK ** -0.5)
        v_all = proj(2 * ML_WIDTH, 3 * ML_WIDTH).astype(BF16)
        o_gate = _sigmoid(proj(3 * ML_WIDTH, 4 * ML_WIDTH))
        zg = ts.zg[...] + b_gate_ref[...]
        fill()

        for c in range(tt // ML_CHUNK):
            r0 = c * ML_CHUNK
            li_s = zg[r0:r0 + ML_CHUNK, :]
            lf_s = _log_sigmoid(_forget_slab(li_s))
            f_s = _cumsum_rows(tri_bf, lf_s)
            a_s = li_s - f_s
            a_t = a_s.T
            for h in range(ML_HEADS):
                hs = slice(h * ML_DK, (h + 1) * ML_DK)
                q = q_all[r0:r0 + ML_CHUNK, hs]
                k = k_all[r0:r0 + ML_CHUNK, hs]
                vaug = jnp.concatenate([v_all[r0:r0 + ML_CHUNK, hs], ones_col], axis=1)
                f_c = f_s[:, h:h + 1]
                a_c = a_s[:, h:h + 1]
                a_r = a_t[h:h + 1, :]
                m_prev = m_scr[h:h + 1, 0:1]
                cmax = jnp.max(jnp.where(tril, a_r, -jnp.inf), axis=1, keepdims=True)
                m_t = f_c + jnp.maximum(m_prev, cmax)
                dmat = jnp.where(tril, jnp.exp((f_c - m_t) + a_r), 0.0)
                s = _dot_nt(q, k.astype(BF16)) * dmat
                dec = jnp.exp(f_c + m_prev - m_t)
                caug = caug_scr[h]
                tot = _dot(s.astype(BF16), vaug) + dec * _dot(q, caug.astype(BF16))
                num = tot[:, :ML_DK]
                qn = tot[:, ML_DK:ML_DK + 1]
                hout = num / jnp.maximum(jnp.abs(qn), jnp.exp(-m_t))
                f_end = f_c[ML_CHUNK - 1:ML_CHUNK, :]
                m_end = m_t[ML_CHUNK - 1:ML_CHUNK, :]
                w_end = jnp.exp(f_end + a_c - m_end)
                dec_end = jnp.exp(f_end + m_prev - m_end)
                caug_scr[h] = dec_end * caug + _dot_tn((k * w_end).astype(BF16), vaug)
                m_scr[h:h + 1, :] = jnp.broadcast_to(m_end, (1, LANES))
                hout = _head_rms(o_gate[r0:r0 + ML_CHUNK, hs] * hout, ml_g_ref[:, hs])
                ts.mixed[r0:r0 + ML_CHUNK, hs] = hout.astype(BF16)
                fill()

        ts.hq[...] = _silu(proj(base, base + HG_WIDTH))
        lf_h, k_h = _hg_gates(proj(base + HG_WIDTH, base + 2 * HG_WIDTH),
                              log_lb_ref[...], log1m_lb_ref[...], one_m_lb_ref[...])
        ts.hk[...] = k_h
        fill()
        ts.hi[...] = proj(base + 2 * HG_WIDTH, base + 3 * HG_WIDTH)
        ts.hgate[...] = _silu(proj(base + 3 * HG_WIDTH, base + 4 * HG_WIDTH))
        ts.hgc[...] = _cumsum_rows(blk_tri, lf_h)
        g_min = None
        for c in range(tt // HG_CHUNK):
            r0 = c * HG_CHUNK
            rows = slice(r0, r0 + HG_CHUNK)
            for h in range(HG_HEADS):
                hs = slice(h * HG_DK, (h + 1) * HG_DK)
                g = ts.hgc[rows, hs]
                q = ts.hq[rows, hs]
                k = ts.hk[rows, hs]
                iv = ts.hi[rows, hs].astype(BF16)
                g_end = g[HG_CHUNK - 1:HG_CHUNK, :]
                g_min = g_end if g_min is None else jnp.minimum(g_min, g_end)
                st = st_scr[h]
                inter = _dot_nt((q * jnp.exp(g)).astype(BF16), st.astype(BF16))
                ref = 0.5 * g_end
                qa = (q * jnp.exp(g - ref)).astype(BF16)
                kb = (k * jnp.exp(ref - g)).astype(BF16)
                amat = jnp.where(tril64, _dot_nt(qa, kb), 0.0)
                ts.hinter[rows, hs] = inter
                ts.ho[rows, hs] = _dot(amat.astype(BF16), iv) + inter
                st_scr[h] = jnp.exp(g_end) * st + _dot_tn(iv, (k * jnp.exp(g_end - g)).astype(BF16))
                if h % 2 == 1:
                    fill()
        while fillers:
            fill()
        return g_min

    def redo_exact(ts, rows_out):
        for c in range(tt // HG_CHUNK):
            r0 = c * HG_CHUNK
            rows = slice(r0, r0 + HG_CHUNK)
            for h in range(HG_HEADS):
                hs = slice(h * HG_DK, (h + 1) * HG_DK)
                g = ts.hgc[rows, hs]
                k = ts.hk[rows, hs]
                ivf = ts.hi[rows, hs]

                def body(grp, carry):
                    rows8 = pl.ds(pl.multiple_of(r0 + grp * SUBLANES, SUBLANES), SUBLANES)
                    g8 = ts.hgc[rows8, hs]
                    q8 = ts.hq[rows8, hs]
                    outs = []
                    for jj in range(SUBLANES):
                        w = jnp.exp(jnp.minimum(g8[jj:jj + 1, :] - g, 0.0)) * k * q8[jj:jj + 1, :]
                        a_col = jnp.sum(w, axis=1, keepdims=True)
                        a_col = jnp.where(row64 <= grp * SUBLANES + jj, a_col, 0.0)
                        outs.append(jnp.sum(a_col * ivf, axis=0, keepdims=True))
                    ts.ho[rows8, hs] = jnp.concatenate(outs, axis=0) + ts.hinter[rows8, hs]
                    return carry

                lax.fori_loop(0, HG_CHUNK // SUBLANES, body, 0)
        finish(ts, rows_out)

    rows_a, rows_b = slice(0, tt), slice(tt, 2 * tt)
    g_min_a = recur(tiles[0], rows_a, projection_pieces(x_ref[0, rows_b, :].astype(BF16), tiles[1]))
    g_min_b = recur(tiles[1], rows_b, [functools.partial(finish, tiles[0], rows_a)]
                    + projection_pieces(xn_ref[0].astype(BF16), tiles[0]))
    finish(tiles[1], rows_b)

    @pl.when(jnp.min(jnp.minimum(g_min_a, g_min_b)) <= -HG_SAFE_DECAY)
    def _():
        redo_exact(tiles[0], rows_a)
        redo_exact(tiles[1], rows_b)

    @pl.when(t == nt - 1)
    def _():
        for h in range(ML_HEADS):
            caug_out_ref[0, h] = caug_scr[h]
            s_out_ref[0, h] = st_scr[h].T
        m_out_ref[0] = m_scr[...]
        conv_out_ref[0] = xp_scr[CONV_PAD - 3:CONV_PAD, :]


class _Layer:
    def __init__(self, array, layer):
        self.array, self.layer = array, layer


def _const_spec(c):
    if isinstance(c, _Layer):
        shape, layer = c.array.shape[1:], c.layer
        return pl.BlockSpec((None,) + shape, lambda *_: (layer,) + (0,) * len(shape), pipeline_mode=pl.Buffered(1))
    nd = c.ndim
    return pl.BlockSpec(c.shape, lambda *_: (0,) * nd, pipeline_mode=pl.Buffered(1))


def _operands(consts):
    return tuple(c.array if isinstance(c, _Layer) else c for c in consts)


def _skip_alias_refs(body, n_in, n_alias, *refs):
    return body(*refs[:n_in], *refs[n_in + n_alias:])


def _alias_args(prev, n_in):
    if prev is None:
        return [], (), {}
    specs = [pl.BlockSpec(memory_space=pl.ANY)] * len(prev)
    return specs, tuple(prev), {n_in + k: 1 + k for k in range(len(prev))}


def _prompt_mixer(x, p, tt, layer, prev):
    bsz, seq, _ = x.shape
    consts = (p["w_a"], p["w_b"], p["w_gate"], p["b_main"], p["b_gate"], p["conv_w"], p["conv_b"], p["ml_g"],
              p["hg_g"], p["log_lb"], p["log1m_lb"], p["one_m_lb"], p["w_out"], p["ln1_g"], p["ln1_b"])
    out_shape = (
        jax.ShapeDtypeStruct((bsz, seq, D_MODEL), F32),
        jax.ShapeDtypeStruct((DEPTH, bsz, CONV_WIDTH - 1, 2 * ML_WIDTH), F32),
        jax.ShapeDtypeStruct((DEPTH, bsz, ML_HEADS, ML_DK, 2 * ML_DK), F32),
        jax.ShapeDtypeStruct((DEPTH, bsz, SUBLANES, LANES), F32),
        jax.ShapeDtypeStruct((DEPTH, bsz, HG_HEADS, HG_DK, HG_DK), F32),
    )
    out_specs = (
        pl.BlockSpec((1, 2 * tt, D_MODEL), lambda b, t: (b, t, 0)),
        pl.BlockSpec((None, 1, CONV_WIDTH - 1, 2 * ML_WIDTH), lambda b, t: (layer, b, 0, 0)),
        pl.BlockSpec((None, 1, ML_HEADS, ML_DK, 2 * ML_DK), lambda b, t: (layer, b, 0, 0, 0)),
        pl.BlockSpec((None, 1, SUBLANES, LANES), lambda b, t: (layer, b, 0, 0)),
        pl.BlockSpec((None, 1, HG_HEADS, HG_DK, HG_DK), lambda b, t: (layer, b, 0, 0, 0)),
    )
    n_in = 2 + len(consts)
    alias_specs, alias_ops, alias_map = _alias_args(prev, n_in)
    tile_scratch = [
        pltpu.VMEM((tt, MAIN_COLS), F32),
        pltpu.VMEM((tt, GATE_COLS), F32),
        pltpu.VMEM((tt, HG_WIDTH), F32),
        pltpu.VMEM((tt, HG_WIDTH), F32),
        pltpu.VMEM((tt, HG_WIDTH), F32),
        pltpu.VMEM((tt, HG_WIDTH), F32),
        pltpu.VMEM((tt, HG_WIDTH), F32),
        pltpu.VMEM((tt, HG_WIDTH), F32),
        pltpu.VMEM((tt, HG_WIDTH), F32),
        pltpu.VMEM((tt, 2 * ML_WIDTH), BF16),
    ]
    assert len(tile_scratch) == _TileScratch.N_REFS
    scratch = [
        pltpu.VMEM((CONV_PAD + tt, 2 * ML_WIDTH), F32),
        pltpu.VMEM((ML_HEADS, ML_DK, 2 * ML_DK), F32),
        pltpu.VMEM((SUBLANES, LANES), F32),
        pltpu.VMEM((HG_HEADS, HG_DK, HG_DK), F32),
    ] + tile_scratch + tile_scratch
    n_pairs = seq // (2 * tt)
    last_pair = bsz * n_pairs - 1

    def next_first_tile(b, t):
        nxt = jnp.minimum(b * n_pairs + t + 1, last_pair)
        return (nxt // n_pairs, 2 * (nxt % n_pairs), 0)

    return pl.pallas_call(
        functools.partial(_skip_alias_refs, functools.partial(_prompt_mixer_pair_kernel, tt=tt), n_in,
                          len(alias_ops)),
        grid=(bsz, n_pairs),
        in_specs=([pl.BlockSpec((1, 2 * tt, D_MODEL), lambda b, t: (b, t, 0)),
                   pl.BlockSpec((1, tt, D_MODEL), next_first_tile)]
                  + [_const_spec(c) for c in consts] + alias_specs),
        out_specs=out_specs,
        out_shape=out_shape,
        scratch_shapes=scratch,
        input_output_aliases=alias_map,
        compiler_params=pltpu.CompilerParams(dimension_semantics=("arbitrary", "arbitrary"),
                                             vmem_limit_bytes=VMEM_LIMIT),
        name="prompt_mixer",
    )(x, x, *_operands(consts), *alias_ops)


DEC_BLOCK = SUBLANES


def _decode_mixer_kernel(x_ref, conv_ref, c_ref, n_ref, m_ref, s_ref,
                         w_a_ref, w_b_ref, w_gate_ref, b_main_ref, b_gate_ref, conv_w_ref, conv_b_ref,
                         ml_g_ref, hg_g_ref, log_lb_ref, log1m_lb_ref, one_m_lb_ref, w_out_ref,
                         ln_g_ref, ln_b_ref,
                         x1_ref, conv_out_ref, c_out_ref, n_out_ref, m_out_ref, s_out_ref,
                         q_scr, k_scr, v_scr, og_scr, li_scr, lf_scr, hq_scr, hk_scr, hf_scr, hi_scr, hgate_scr,
                         qt_scr, kt_scr, hqt_scr, hkt_scr, hft_scr, hml_scr, ho_scr, *, nb):
    i = pl.program_id(0)
    nsteps = pl.num_programs(0)

    @pl.when(i == 0)
    def _():
        xb = x_ref[...].astype(BF16)

        def proj(lo, hi):
            return _dot(xb, _main_weight(w_a_ref, w_b_ref, lo, hi)) + b_main_ref[:, lo:hi]

        qk_raw = proj(0, 2 * ML_WIDTH)
        w = 2 * ML_WIDTH
        qk = conv_b_ref[...] + qk_raw * conv_w_ref[CONV_WIDTH - 1:CONV_WIDTH, :]
        for j in range(CONV_WIDTH - 1):
            qk = qk + conv_ref[:, j * w:(j + 1) * w] * conv_w_ref[j:j + 1, :]
        qk = _silu(qk)
        conv_out_ref[:, 0:2 * w] = conv_ref[:, w:3 * w]
        conv_out_ref[:, 2 * w:3 * w] = qk_raw
        q_scr[...] = qk[:, :ML_WIDTH]
        k_scr[...] = qk[:, ML_WIDTH:] * (ML_DK ** -0.5)
        v_scr[...] = proj(2 * ML_WIDTH, 3 * ML_WIDTH)
        og_scr[...] = _sigmoid(proj(3 * ML_WIDTH, 4 * ML_WIDTH))
        zg = _dot(xb, w_gate_ref[...]) + b_gate_ref[...]
        li_scr[...] = zg
        lf_scr[...] = _log_sigmoid(_forget_slab(zg))
        base = 4 * ML_WIDTH
        hq_scr[...] = _silu(proj(base, base + HG_WIDTH))
        lf_h, k_h = _hg_gates(proj(base + HG_WIDTH, base + 2 * HG_WIDTH),
                              log_lb_ref[...], log1m_lb_ref[...], one_m_lb_ref[...])
        hk_scr[...] = k_h
        hf_scr[...] = jnp.exp(lf_h)
        hi_scr[...] = proj(base + 2 * HG_WIDTH, base + 3 * HG_WIDTH)
        hgate_scr[...] = _silu(proj(base + 3 * HG_WIDTH, base + 4 * HG_WIDTH))
        def split_t(blk):
            xt = blk.T
            hi = xt.astype(BF16)
            return jnp.concatenate([hi, (xt - hi.astype(F32)).astype(BF16)], axis=1)

        for h in range(ML_HEADS):
            hs = slice(h * ML_DK, (h + 1) * ML_DK)
            qt_scr[h] = split_t(q_scr[:, hs])
            kt_scr[h] = split_t(k_scr[:, hs])
            hqt_scr[h] = split_t(hq_scr[:, hs])
            hkt_scr[h] = split_t(hk_scr[:, hs])
            hft_scr[h] = split_t(hf_scr[:, hs])

    b0 = pl.multiple_of(i * DEC_BLOCK, DEC_BLOCK)
    rows = pl.ds(b0, DEC_BLOCK)
    pick_seq = lax.broadcasted_iota(jnp.int32, (2 * nb, DEC_BLOCK * LANES), 0) & (nb - 1)
    pick_slab = lax.broadcasted_iota(jnp.int32, (2 * nb, DEC_BLOCK * LANES), 1) >> 7
    spread = jnp.where(pick_seq == b0 + pick_slab, 1.0, 0.0).astype(BF16)
    li = li_scr[rows, :]
    lf = lf_scr[rows, :]
    m_prev = m_ref[...]
    q_blk = q_scr[rows, :]
    k_blk = k_scr[rows, :]
    v_blk = v_scr[rows, :]
    hi_blk = hi_scr[rows, :]
    n_blk = n_ref[...]
    m_new_cols = []
    for h in range(ML_HEADS):
        hs = slice(h * ML_DK, (h + 1) * ML_DK)
        li_h = li[:, h:h + 1]
        lf_h = lf[:, h:h + 1]
        mp_h = m_prev[:, h:h + 1]
        m_t = jnp.maximum(lf_h + mp_h, li_h)
        dmat = jnp.exp(li_h - m_t)
        dec = jnp.exp(lf_h + mp_h - m_t)
        n_h = n_blk[:, hs]
        s = jnp.sum(q_blk[:, hs] * k_blk[:, hs], axis=1, keepdims=True) * dmat
        qn = s + dec * jnp.sum(q_blk[:, hs] * n_h, axis=1, keepdims=True)
        denom = jnp.maximum(jnp.abs(qn), jnp.exp(-m_t))
        n_out_ref[:, hs] = dec * n_h + dmat * k_blk[:, hs]
        m_new_cols.append(m_t)
        qt = _dot(qt_scr[h], spread)
        kt = _dot(kt_scr[h], spread)
        hqt = _dot(hqt_scr[h], spread)
        hkt = _dot(hkt_scr[h], spread)
        hft = _dot(hft_scr[h], spread)
        qc_rows, ho_rows = [], []
        for j in range(DEC_BLOCK):
            jr = slice(j, j + 1)
            js = slice(j * LANES, (j + 1) * LANES)
            cmat = c_ref[j, h]
            qc_rows.append(jnp.sum(qt[:, js] * cmat, axis=0, keepdims=True))
            c_out_ref[j, h] = dec[jr, :] * cmat + (dmat[jr, :] * kt[:, js]) * v_blk[jr, hs]
            s_new = hft[:, js] * s_ref[j, h] + hkt[:, js] * hi_blk[jr, hs]
            s_out_ref[j, h] = s_new
            ho_rows.append(jnp.sum(hqt[:, js] * s_new, axis=0, keepdims=True))
        num = s * v_blk[:, hs] + dec * jnp.concatenate(qc_rows, axis=0)
        hml_scr[rows, hs] = num / denom
        ho_scr[rows, hs] = jnp.concatenate(ho_rows, axis=0)
    m_out_ref[...] = jnp.concatenate(m_new_cols, axis=1)

    @pl.when(i == nsteps - 1)
    def _():
        for h in range(ML_HEADS):
            hs = slice(h * ML_DK, (h + 1) * ML_DK)
            hm = _head_rms(og_scr[:, hs] * hml_scr[:, hs], ml_g_ref[:, hs])
            hml_scr[:, hs] = hm
            ho = _head_rms(ho_scr[:, hs], hg_g_ref[:, hs]) * hgate_scr[:, hs]
            ho_scr[:, hs] = ho
        mix = (_dot(hml_scr[...].astype(BF16), w_out_ref[0:ML_WIDTH, :])
               + _dot(ho_scr[...].astype(BF16), w_out_ref[ML_WIDTH:2 * ML_WIDTH, :]))
        x1_ref[...] = _layer_norm_rows(DEEPNORM_ALPHA * x_ref[...] + mix, ln_g_ref[...], ln_b_ref[...])


def _decode_mixer(x, conv, c_state, n_state, m_state, s_state, p, layer, prev):
    nb = x.shape[0]
    assert nb == LANES, "decode batch must fill one lane tile"
    consts = (p["w_a"], p["w_b"], p["w_gate"], p["b_main"], p["b_gate"], p["conv_w"], p["conv_b"], p["ml_g"],
              p["hg_g"], p["log_lb"], p["log1m_lb"], p["one_m_lb"], p["w_out"], p["ln1_g"], p["ln1_b"])
    conv_cols = (CONV_WIDTH - 1) * 2 * ML_WIDTH
    blk4 = lambda i: (layer, i, 0, 0, 0)
    blk2 = lambda i: (layer, i, 0)
    conv_spec = pl.BlockSpec((None, nb, conv_cols), lambda i: (layer, 0, 0))
    state_specs = [
        pl.BlockSpec((None, DEC_BLOCK, ML_HEADS, ML_DK, ML_DK), blk4),
        pl.BlockSpec((None, DEC_BLOCK, ML_WIDTH), blk2),
        pl.BlockSpec((None, DEC_BLOCK, ML_HEADS), blk2),
        pl.BlockSpec((None, DEC_BLOCK, HG_HEADS, HG_DK, HG_DK), blk4),
    ]
    full2 = lambda shape: pl.BlockSpec(shape, lambda i: (0, 0))
    out_shape = (
        jax.ShapeDtypeStruct((nb, D_MODEL), F32),
        jax.ShapeDtypeStruct(conv.shape, F32),
        jax.ShapeDtypeStruct(c_state.shape, F32),
        jax.ShapeDtypeStruct(n_state.shape, F32),
        jax.ShapeDtypeStruct(m_state.shape, F32),
        jax.ShapeDtypeStruct(s_state.shape, F32),
    )
    n_in = 6 + len(consts)
    alias_specs, alias_ops, alias_map = _alias_args(prev, n_in)
    rows_f32 = lambda w: pltpu.VMEM((nb, w), F32)
    scratch = ([rows_f32(ML_WIDTH)] * 4 + [rows_f32(LANES)] * 2 + [rows_f32(HG_WIDTH)] * 5
               + [pltpu.VMEM((ML_HEADS, ML_DK, 2 * nb), BF16)] * 5 + [rows_f32(ML_WIDTH), rows_f32(HG_WIDTH)])
    return pl.pallas_call(
        functools.partial(_skip_alias_refs, functools.partial(_decode_mixer_kernel, nb=nb), n_in, len(alias_ops)),
        grid=(nb // DEC_BLOCK,),
        in_specs=([full2((nb, D_MODEL)), conv_spec] + state_specs + [_const_spec(c) for c in consts]
                  + alias_specs),
        out_specs=(full2((nb, D_MODEL)), conv_spec) + tuple(state_specs),
        out_shape=out_shape,
        scratch_shapes=scratch,
        input_output_aliases=alias_map,
        compiler_params=pltpu.CompilerParams(dimension_semantics=("arbitrary",), vmem_limit_bytes=VMEM_LIMIT),
        name="decode_mixer",
    )(x, conv, c_state, n_state, m_state, s_state, *_operands(consts), *alias_ops)


def _moe_kernel(x_ref, w_r_ref, b_r_ref, w1_ref, w3_ref, w2_ref, ln_g_ref, ln_b_ref, y_ref, hdn_scr):
    x = x_ref[...]
    xb = x.astype(BF16)
    logits = _dot(xb, w_r_ref[...]) + b_r_ref[...]
    lg = logits[:, 0:N_GROUPS]
    le = logits[:, N_GROUPS:N_GROUPS + N_EXPERTS]
    gidx = lax.broadcasted_iota(jnp.int32, lg.shape, 1)
    gmax = jnp.max(lg, axis=1, keepdims=True)
    g = jnp.min(jnp.where(lg == gmax, gidx, N_GROUPS), axis=1, keepdims=True)
    pg = 1.0 / jnp.sum(jnp.exp(lg - gmax), axis=1, keepdims=True)
    eidx = lax.broadcasted_iota(jnp.int32, le.shape, 1)
    in_g = (eidx >> 2) == g
    lm = jnp.where(in_g, le, -jnp.inf)
    v1 = jnp.max(lm, axis=1, keepdims=True)
    i1 = jnp.min(jnp.where(lm == v1, eidx, N_EXPERTS), axis=1, keepdims=True)
    lm2 = jnp.where(eidx == i1, -jnp.inf, lm)
    v2 = jnp.max(lm2, axis=1, keepdims=True)
    i2 = jnp.min(jnp.where(lm2 == v2, eidx, N_EXPERTS), axis=1, keepdims=True)
    e21 = jnp.exp(v2 - v1)
    wt1 = pg / (1.0 + e21)
    wt2 = wt1 * e21
    gate = jnp.where(eidx == i1, wt1, 0.0) + jnp.where(eidx == i2, wt2, 0.0)
    for e in range(N_EXPERTS):
        a = _dot(xb, w1_ref[e])
        u = _dot(xb, w3_ref[e])
        hdn = _silu(a) * u * gate[:, e:e + 1]
        hdn_scr[:, e * EXPERT_FF:(e + 1) * EXPERT_FF] = hdn.astype(BF16)
    moe = _dot(hdn_scr[...], w2_ref[...])
    y_ref[...] = _layer_norm_rows(DEEPNORM_ALPHA * x + moe, ln_g_ref[...], ln_b_ref[...])


def _moe(x, p, tm):
    n = x.shape[0]
    consts = (p["w_router"], p["b_router"], p["w_gate_e"], p["w_up_e"], p["w_down_e"], p["ln2_g"], p["ln2_b"])
    return pl.pallas_call(
        _moe_kernel,
        grid=(n // tm,),
        in_specs=[pl.BlockSpec((tm, D_MODEL), lambda i: (i, 0))] + [_const_spec(c) for c in consts],
        out_specs=pl.BlockSpec((tm, D_MODEL), lambda i: (i, 0)),
        out_shape=jax.ShapeDtypeStruct((n, D_MODEL), F32),
        scratch_shapes=[pltpu.VMEM((tm, N_EXPERTS * EXPERT_FF), BF16)],
        compiler_params=pltpu.CompilerParams(dimension_semantics=("arbitrary",), vmem_limit_bytes=VMEM_LIMIT),
        name="moe",
    )(x, *_operands(consts))


MOE_BLOCK = 128
GROUP_FF = EXPERTS_PER_GROUP * EXPERT_FF


def _moe_grouped_kernel(x_ref, w_rt_ref, b_rt_ref, w1_ref, w3_ref, w2_ref, ln_g_ref, ln_b_ref, y_ref,
                        xb_scr, gt_scr, pos_scr, acc_scr, *, tm):
    x = x_ref[...]
    xb = x.astype(BF16)
    xb_scr[...] = xb
    logits = _dot_nt(w_rt_ref[...], xb) + b_rt_ref[...]
    ridx8 = lax.broadcasted_iota(jnp.int32, (SUBLANES, tm), 0)
    lg = jnp.where(ridx8 < N_GROUPS, logits[N_EXPERTS:N_EXPERTS + SUBLANES, :], -jnp.inf)
    gmax = jnp.max(lg, axis=0, keepdims=True)
    g = jnp.min(jnp.where(lg == gmax, ridx8, SUBLANES), axis=0, keepdims=True)
    pg = 1.0 / jnp.sum(jnp.exp(lg - gmax), axis=0, keepdims=True)
    le = logits[0:N_EXPERTS, :]
    eidx = lax.broadcasted_iota(jnp.int32, (N_EXPERTS, tm), 0)
    lm = jnp.where((eidx >> 2) == g, le, -jnp.inf)
    v1 = jnp.max(lm, axis=0, keepdims=True)
    i1 = jnp.min(jnp.where(lm == v1, eidx, N_EXPERTS), axis=0, keepdims=True)
    lm2 = jnp.where(eidx == i1, -jnp.inf, lm)
    v2 = jnp.max(lm2, axis=0, keepdims=True)
    i2 = jnp.min(jnp.where(lm2 == v2, eidx, N_EXPERTS), axis=0, keepdims=True)
    e21 = jnp.exp(v2 - v1)
    wt1 = pg / (1.0 + e21)
    wt2 = wt1 * e21
    gate = jnp.where(eidx == i1, wt1, 0.0) + jnp.where(eidx == i2, wt2, 0.0)
    gate4 = gate[0:4, :] + gate[4:8, :] + gate[8:12, :] + gate[12:16, :]
    g_hi = gate4.astype(BF16)
    g_lo = (gate4 - g_hi.astype(F32)).astype(BF16)
    gt_scr[...] = jnp.zeros_like(gt_scr)
    gt_scr[0:4, :] = g_hi
    gt_scr[SUBLANES:SUBLANES + 4, :] = g_lo

    onehot = jnp.where(ridx8 == g, 1.0, 0.0)
    tr = lax.broadcasted_iota(jnp.int32, (tm, tm), 0)
    tc = lax.broadcasted_iota(jnp.int32, (tm, tm), 1)
    before = jnp.where(tr < tc, 1.0, 0.0).astype(BF16)
    rank = _dot(onehot.astype(BF16), before)
    counts = jnp.sum(onehot, axis=1, keepdims=True).astype(jnp.int32)
    nblk_g = (counts + (MOE_BLOCK - 1)) >> (MOE_BLOCK.bit_length() - 1)
    s1 = nblk_g[0:1, :]
    s2 = s1 + nblk_g[1:2, :]
    s3 = s2 + nblk_g[2:3, :]
    n_blocks = s3 + nblk_g[3:4, :]
    cidx = lax.broadcasted_iota(jnp.int32, (SUBLANES, 1), 0)
    start_blk = jnp.where(cidx == 1, s1, 0) + jnp.where(cidx == 2, s2, 0) + jnp.where(cidx == 3, s3, 0)
    pos = jnp.sum(onehot * ((start_blk * MOE_BLOCK).astype(F32) + rank), axis=0, keepdims=True)
    pos_scr[...] = jnp.broadcast_to(pos, (SUBLANES, tm))
    acc_scr[...] = jnp.zeros_like(acc_scr)
    b1 = s1[0, 0]
    b2 = s2[0, 0]
    b3 = s3[0, 0]

    def block(b, carry):
        grp = jnp.where(b >= b1, 1, 0) + jnp.where(b >= b2, 1, 0) + jnp.where(b >= b3, 1, 0)
        slot = lax.broadcasted_iota(jnp.int32, (MOE_BLOCK, 1), 0) + b * MOE_BLOCK
        sel = jnp.where(pos_scr[0:1, :] == slot.astype(F32), 1.0, 0.0).astype(BF16)
        xs = _dot(sel, xb_scr[...]).astype(BF16)
        gs = _dot_nt(sel, gt_scr[...])
        gsel = gs[:, 0:4] + gs[:, SUBLANES:SUBLANES + 4]
        parts = []
        for j in range(EXPERTS_PER_GROUP):
            e = grp * EXPERTS_PER_GROUP + j
            a = _dot(xs, w1_ref[e])
            u = _dot(xs, w3_ref[e])
            parts.append((_silu(a) * u * gsel[:, j:j + 1]).astype(BF16))
        hdn = jnp.concatenate(parts, axis=1)
        ys = _dot(hdn, w2_ref[pl.ds(pl.multiple_of(grp * GROUP_FF, GROUP_FF), GROUP_FF), :])
        y_hi = ys.astype(BF16)
        y_lo = (ys - y_hi.astype(F32)).astype(BF16)
        acc_scr[...] += _dot_tn(jnp.concatenate([sel, sel], axis=0), jnp.concatenate([y_hi, y_lo], axis=0))
        return carry

    lax.fori_loop(0, n_blocks[0, 0], block, 0)
    y_ref[...] = _layer_norm_rows(DEEPNORM_ALPHA * x + acc_scr[...], ln_g_ref[...], ln_b_ref[...])


def _moe_grouped(x, p, tm):
    n = x.shape[0]
    consts = (p["w_router_t"], p["b_router_t"], p["w_gate_e"], p["w_up_e"], p["w_down_e"], p["ln2_g"], p["ln2_b"])
    return pl.pallas_call(
        functools.partial(_moe_grouped_kernel, tm=tm),
        grid=(n // tm,),
        in_specs=[pl.BlockSpec((tm, D_MODEL), lambda i: (i, 0))] + [_const_spec(c) for c in consts],
        out_specs=pl.BlockSpec((tm, D_MODEL), lambda i: (i, 0)),
        out_shape=jax.ShapeDtypeStruct((n, D_MODEL), F32),
        scratch_shapes=[pltpu.VMEM((tm, D_MODEL), BF16),
                        pltpu.VMEM((2 * SUBLANES, tm), BF16),
                        pltpu.VMEM((SUBLANES, tm), F32),
                        pltpu.VMEM((tm, D_MODEL), F32)],
        compiler_params=pltpu.CompilerParams(dimension_semantics=("arbitrary",), vmem_limit_bytes=VMEM_LIMIT),
        name="moe_grouped",
    )(x, *_operands(consts))


def _stacked_weights(w_in, w_out, w_gate, w_up, w_down):
    g0 = 4 * ML_WIDTH
    w_in_bf = w_in.astype(BF16)
    return {
        "w_a": w_in_bf[:, :, :g0],
        "w_b": w_in_bf[:, :, ML_COLS:],
        "w_gate": jnp.pad(w_in_bf[:, :, g0:ML_COLS], ((0, 0), (0, 0), (0, LANES - 2 * ML_HEADS))),
        "w_out": w_out.astype(BF16),
        "w_gate_e": w_gate.astype(BF16), "w_up_e": w_up.astype(BF16),
        "w_down_e": w_down.astype(BF16).reshape(DEPTH, N_EXPERTS * EXPERT_FF, D_MODEL),
    }


def _layer_params(l, lbs, stacked, b_in, conv_w, conv_b, ml_norm_g, hg_norm_g, ln1_g, ln1_b,
                  w_router_group, b_router_group, w_router_expert, b_router_expert, ln2_g, ln2_b):
    g0 = 4 * ML_WIDTH
    b = b_in[l]
    b_main = jnp.concatenate([b[:g0], b[ML_COLS:]])[None, :]
    b_gate_cols = jnp.pad(b[g0:ML_COLS], (0, LANES - 2 * ML_HEADS))[None, :]
    pad_r = LANES - N_GROUPS - N_EXPERTS
    w_router = jnp.concatenate([w_router_group[l], w_router_expert[l], jnp.zeros((D_MODEL, pad_r), F32)],
                               axis=1).astype(BF16)
    b_router = jnp.concatenate([b_router_group[l], b_router_expert[l], jnp.zeros((pad_r,), F32)])[None, :]
    w_router_t = jnp.concatenate([w_router_expert[l], w_router_group[l], jnp.zeros((D_MODEL, pad_r), F32)],
                                 axis=1).T.astype(BF16)
    b_router_t = jnp.concatenate([b_router_expert[l], b_router_group[l], jnp.zeros((pad_r,), F32)])[:, None]
    lb = lbs[l]
    p = {name: _Layer(array, l) for name, array in stacked.items()}
    p.update({
        "w_router_t": w_router_t, "b_router_t": b_router_t,
        "b_main": b_main, "b_gate": b_gate_cols,
        "conv_w": conv_w[l], "conv_b": conv_b[l][None, :], "ml_g": ml_norm_g[l][None, :],
        "hg_g": hg_norm_g[l][None, :], "log_lb": jnp.log(lb)[None, :], "log1m_lb": jnp.log1p(-lb)[None, :],
        "one_m_lb": (1.0 - lb)[None, :],
        "ln1_g": ln1_g[l][None, :], "ln1_b": ln1_b[l][None, :],
        "w_router": w_router, "b_router": b_router,
        "ln2_g": ln2_g[l][None, :], "ln2_b": ln2_b[l][None, :],
    })
    return p


def _pick_tile(n, pref):
    t = pref
    while n % t:
        t //= 2
    return t


def kernel(x_prompt, x_sample, state_conv, state_ml_C, state_ml_n, state_ml_m, state_hg_S, w_in, b_in, conv_w, conv_b, ml_norm_g, hg_lower_bounds, hg_norm_g, w_out, ln1_g, ln1_b, w_router_group, b_router_group, w_router_expert, b_router_expert, w_gate, w_up, w_down, ln2_g, ln2_b):
    sm = jax.nn.softmax(hg_lower_bounds.astype(F32), axis=0)
    lbs = jnp.concatenate([jnp.zeros_like(sm[:1]), jnp.cumsum(sm[1:], axis=0)], axis=0)
    bsz, seq, _ = x_prompt.shape
    nb = x_sample.shape[0]
    tt = _pick_tile(seq, 256)
    y_p = x_prompt
    y_s = x_sample.reshape(nb, D_MODEL)
    n_tok = bsz * seq
    conv_in = state_conv.reshape(DEPTH, nb, -1)
    n_in = state_ml_n.reshape(DEPTH, nb, ML_WIDTH)
    st_p = st_s = None
    stacked = _stacked_weights(w_in, w_out, w_gate, w_up, w_down)
    for l in range(DEPTH):
        p = _layer_params(l, lbs, stacked, b_in, conv_w, conv_b, ml_norm_g, hg_norm_g, ln1_g, ln1_b,
                          w_router_group, b_router_group, w_router_expert, b_router_expert, ln2_g, ln2_b)
        x1, *st_p = _prompt_mixer(y_p, p, tt, l, st_p)
        y_p = _moe_grouped(x1.reshape(n_tok, D_MODEL), p, _pick_tile(n_tok, 512)).reshape(bsz, seq, D_MODEL)
        x1s, *st_s = _decode_mixer(y_s, conv_in, state_ml_C, n_in, state_ml_m, state_hg_S, p, l, st_s)
        y_s = _moe(x1s, p, _pick_tile(nb, 256))
    conv_p, caug, m_p, s_p = st_p
    conv_s, c_s, n_s, m_s, s_s = st_s
    return (y_p, y_s.reshape(nb, 1, D_MODEL),
            conv_p, caug[..., :ML_DK], caug[..., ML_DK], m_p[:, :, :ML_HEADS, 0], s_p,
            conv_s.reshape(DEPTH, nb, CONV_WIDTH - 1, 2 * ML_WIDTH), c_s,
            n_s.reshape(DEPTH, nb, ML_HEADS, ML_DK), m_s, s_s)
```
